```python
import numpy as np
import jax
import jax.numpy as jnp
from jax import lax

D_MODEL = 2048
BATCH = 2
SEQ = 8192
DEPTH = 4

NORM_EPS = 1e-6
ROPE_THETA = 500000.0
Q_BLOCK = 128

NSA_HEAD_DIM = 128
NSA_HEADS = D_MODEL // 2 // NSA_HEAD_DIM
NSA_KV_GROUPS = 2
NSA_HEADS_PER_GROUP = NSA_HEADS // NSA_KV_GROUPS
NSA_ROT_DIM = NSA_HEAD_DIM // 4
N_NSA_BRANCHES = 3
CMP_BLOCK = 32
CMP_STRIDE = 16
SEL_BLOCK = 64
N_SEL_BLOCKS = 16
WINDOW = 512
SEL_FORCE = 1.0e4

MLA_V_DIM = 128
MLA_HEADS = D_MODEL // 2 // MLA_V_DIM
MLA_Q_RANK = 512
MLA_KV_RANK = 512
MLA_NOPE_DIM = 128
MLA_ROPE_DIM = 64
MLA_QK_DIM = MLA_NOPE_DIM + MLA_ROPE_DIM

NSA_Q_COLS = NSA_HEADS * NSA_HEAD_DIM
NSA_KV_COLS = N_NSA_BRANCHES * 2 * NSA_KV_GROUPS * NSA_HEAD_DIM
NSA_GATE_COLS = NSA_HEADS * N_NSA_BRANCHES
MLA_QA_COLS = MLA_Q_RANK
MLA_KVA_COLS = MLA_KV_RANK + MLA_ROPE_DIM
D_IN = NSA_Q_COLS + NSA_KV_COLS + NSA_GATE_COLS + MLA_QA_COLS + MLA_KVA_COLS
SPLITS = (NSA_Q_COLS, NSA_Q_COLS + NSA_KV_COLS, NSA_Q_COLS + NSA_KV_COLS + NSA_GATE_COLS,
          NSA_Q_COLS + NSA_KV_COLS + NSA_GATE_COLS + MLA_QA_COLS)
NSA_OUT = NSA_HEADS * NSA_HEAD_DIM
MLA_OUT = MLA_HEADS * MLA_V_DIM
MIX_WIDTH = NSA_OUT + MLA_OUT

D_FF = 5632
N_EXPERTS = 8
TOP_K = 2
MOE_BLOCK = 256

kernel_name = "hybrid_nsa_mla_moe_trunk"


def rms_norm(x, g):
    xf = x.astype(jnp.float32)
    y = xf * lax.rsqrt(jnp.mean(xf * xf, axis=-1, keepdims=True) + NORM_EPS)
    return (y * g.astype(jnp.float32)).astype(x.dtype)


def rope_tables(pos, rot_dim):
    inv_freq = jnp.power(jnp.float32(ROPE_THETA), -jnp.arange(0, rot_dim, 2, dtype=jnp.float32) / rot_dim)
    ang = pos.astype(jnp.float32)[..., None] * inv_freq
    return jnp.cos(ang), jnp.sin(ang)


def apply_rope(x, cos, sin):
    half = cos.shape[-1]
    shape = cos.shape[:2] + (1,) * (x.ndim - 3) + (half,)
    c = cos.reshape(shape)
    s = sin.reshape(shape)
    x1 = x[..., :half].astype(jnp.float32)
    x2 = x[..., half:2 * half].astype(jnp.float32)
    rot = jnp.concatenate([x1 * c - x2 * s, x2 * c + x1 * s], axis=-1).astype(x.dtype)
    return jnp.concatenate([rot, x[..., 2 * half:]], axis=-1)


def masked_softmax(s, mask):
    p = jax.nn.softmax(jnp.where(mask, s.astype(jnp.float32), -1e30), axis=-1)
    return jnp.where(mask, p, 0.0)


def swiglu(h, wg, wu, wd):
    return (jax.nn.silu(h @ wg) * (h @ wu)) @ wd


def compress(t, w1, w2, pos_emb):
    B, S, G, Dh = t.shape
    nc = (S - CMP_BLOCK) // CMP_STRIDE + 1
    idx = np.arange(nc)[:, None] * CMP_STRIDE + np.arange(CMP_BLOCK)[None, :]
    blocks = t[:, idx] + pos_emb[:, None, :]
    blocks = blocks.transpose(0, 1, 3, 2, 4).reshape(B, nc, G, CMP_BLOCK * Dh)
    return jax.nn.gelu(blocks @ w1) @ w2


def nsa_attention(q, k_cmp, v_cmp, k_slc, v_slc, k_win, v_win, gates):
    B, S, G, Hg, Dh = q.shape
    nc = k_cmp.shape[1]
    n_blk = S // SEL_BLOCK
    n_sel = min(N_SEL_BLOCKS, n_blk)
    scale = Dh ** -0.5
    cmp_start = np.arange(nc) * CMP_STRIDE
    cmp_end = cmp_start + CMP_BLOCK - 1
    blk_start = np.arange(n_blk) * SEL_BLOCK
    overlap = ((cmp_start[:, None] < blk_start[None, :] + SEL_BLOCK)
               & (cmp_end[:, None] >= blk_start[None, :])).astype(np.float32)
    ks_blk = k_slc.reshape(B, n_blk, SEL_BLOCK, G, Dh).transpose(0, 3, 1, 2, 4)
    vs_blk = v_slc.reshape(B, n_blk, SEL_BLOCK, G, Dh).transpose(0, 3, 1, 2, 4)
    kw_pad = jnp.pad(k_win, ((0, 0), (WINDOW, 0), (0, 0), (0, 0)))
    vw_pad = jnp.pad(v_win, ((0, 0), (WINDOW, 0), (0, 0), (0, 0)))
    gather_blocks = jax.vmap(jax.vmap(lambda blocks, idx: blocks[idx]))
    j = jnp.arange(n_blk)

    def query_block(i):
        s0 = i * Q_BLOCK
        t = s0 + jnp.arange(Q_BLOCK)
        qb = lax.dynamic_slice_in_dim(q, s0, Q_BLOCK, axis=1)
        gb = lax.dynamic_slice_in_dim(gates, s0, Q_BLOCK, axis=1)
        s_c = jnp.einsum('bqghd,bngd->bqghn', qb, k_cmp) * scale
        m_c = (cmp_end[None, :] <= t[:, None])[None, :, None, None, :]
        p_c = masked_softmax(s_c, m_c)
        o_c = jnp.einsum('bqghn,bngd->bqghd', p_c.astype(v_cmp.dtype), v_cmp)
        imp = jnp.einsum('bqgn,nj->bqgj', p_c.sum(axis=3), overlap)
        cur = t // SEL_BLOCK
        eligible = (j[None, :] <= cur[:, None])[None, :, None, :]
        forced = ((j[None, :] == 0) | (j[None, :] == cur[:, None])
                  | (j[None, :] == cur[:, None] - 1))[None, :, None, :]
        imp = jnp.where(eligible, imp + jnp.where(forced, SEL_FORCE, 0.0), -jnp.inf)
        top_val, top_idx = lax.top_k(imp, n_sel)
        idx_g = top_idx.transpose(0, 2, 1, 3)
        k_sel = gather_blocks(ks_blk, idx_g).reshape(B, G, Q_BLOCK, n_sel * SEL_BLOCK, Dh)
        v_sel = gather_blocks(vs_blk, idx_g).reshape(B, G, Q_BLOCK, n_sel * SEL_BLOCK, Dh)
        kpos = (top_idx[..., None] * SEL_BLOCK + jnp.arange(SEL_BLOCK)).reshape(B, Q_BLOCK, G, n_sel * SEL_BLOCK)
        valid = jnp.repeat(jnp.isfinite(top_val), SEL_BLOCK, axis=-1)
        m_s = (valid & (kpos <= t[None, :, None, None]))[:, :, :, None, :]
        s_s = jnp.einsum('bqghd,bgqkd->bqghk', qb, k_sel) * scale
        p_s = masked_softmax(s_s, m_s)
        o_s = jnp.einsum('bqghk,bgqkd->bqghd', p_s.astype(v_sel.dtype), v_sel)
        kw = lax.dynamic_slice_in_dim(kw_pad, s0, Q_BLOCK + WINDOW, axis=1)
        vw = lax.dynamic_slice_in_dim(vw_pad, s0, Q_BLOCK + WINDOW, axis=1)
        kpos_w = s0 - WINDOW + jnp.arange(Q_BLOCK + WINDOW)
        diff = t[:, None] - kpos_w[None, :]
        m_w = ((kpos_w[None, :] >= 0) & (diff >= 0) & (diff < WINDOW))[None, :, None, None, :]
        s_w = jnp.einsum('bqghd,bkgd->bqghk', qb, kw) * scale
        p_w = masked_softmax(s_w, m_w)
        o_w = jnp.einsum('bqghk,bkgd->bqghd', p_w.astype(vw.dtype), vw)
        return gb[..., 0:1] * o_c + gb[..., 1:2] * o_s + gb[..., 2:3] * o_w

    out = lax.map(query_block, jnp.arange(S // Q_BLOCK))
    return jnp.moveaxis(out, 0, 1).reshape(B, S, G * Hg * Dh)


def causal_attention(q, k, v):
    B, S, H, Dq = q.shape
    scale = Dq ** -0.5
    kpos = jnp.arange(S)

    def query_block(i):
        s0 = i * Q_BLOCK
        t = s0 + jnp.arange(Q_BLOCK)
        qb = lax.dynamic_slice_in_dim(q, s0, Q_BLOCK, axis=1)
        s = jnp.einsum('bqhd,bkhd->bhqk', qb, k) * scale
        p = masked_softmax(s, (kpos[None, :] <= t[:, None])[None, None])
        return jnp.einsum('bhqk,bkhd->bqhd', p.astype(v.dtype), v)

    out = lax.map(query_block, jnp.arange(S // Q_BLOCK))
    return jnp.moveaxis(out, 0, 1).reshape(B, S, H * v.shape[-1])


def moe_swiglu(h, w_router, w_gate, w_up, w_down):
    T, D = h.shape
    n_assign = T * TOP_K
    logits = (h @ w_router).astype(jnp.float32)
    top_logit, top_e = lax.top_k(logits, TOP_K)
    gate = jax.nn.softmax(top_logit, axis=-1)
    flat_e = top_e.reshape(-1)
    order = jnp.argsort(flat_e)
    se = flat_e[order]
    stok = (order // TOP_K).astype(jnp.int32)
    sg = gate.reshape(-1)[order]
    counts = jnp.bincount(flat_e, length=N_EXPERTS)
    padded = (counts + MOE_BLOCK - 1) // MOE_BLOCK * MOE_BLOCK
    pad_end = jnp.cumsum(padded)
    pad_start = pad_end - padded
    start = jnp.cumsum(counts) - counts
    dest = pad_start[se] + jnp.arange(n_assign) - start[se]
    n_blocks = -(-(n_assign + N_EXPERTS * (MOE_BLOCK - 1)) // MOE_BLOCK)
    n_rows = n_blocks * MOE_BLOCK
    row_tok = jnp.zeros((n_rows,), jnp.int32).at[dest].set(stok)
    x_rows = h[row_tok].reshape(n_blocks, MOE_BLOCK, D)
    block_e = jnp.minimum(jnp.searchsorted(pad_end, jnp.arange(n_blocks) * MOE_BLOCK, side='right'), N_EXPERTS - 1)

    def expert_block(args):
        xb, e = args
        return swiglu(xb, w_gate[e], w_up[e], w_down[e])

    y_rows = lax.map(expert_block, (x_rows, block_e)).reshape(n_rows, D)
    y = y_rows[dest] * sg[:, None].astype(h.dtype)
    return jnp.zeros_like(h).at[stok].add(y)


def setup_inputs(seed: int = 0) -> dict:
    key = jax.random.key(seed)
    keys = iter(jax.random.split(key, 32))
    f32 = jnp.float32
    L, D, Dh = DEPTH, D_MODEL, NSA_HEAD_DIM
    n_dense = (DEPTH + 1) // 2
    n_moe = DEPTH // 2

    def w(shape, fan_in):
        return jax.random.normal(next(keys), shape, f32) * fan_in ** -0.5

    def gain(shape):
        return 1.0 + 0.02 * jax.random.normal(next(keys), shape, f32)

    x = jax.random.normal(next(keys), (BATCH, SEQ, D), f32)
    start = jax.random.randint(next(keys), (BATCH, 1), 0, 4096, dtype=jnp.int32)
    positions = start + jnp.arange(SEQ, dtype=jnp.int32)[None, :]
    return {
        'x': x,
        'positions': positions,
        'attn_norm': gain((L, D)),
        'w_in': w((L, D, D_IN), D),
        'nsa_q_norm': gain((L, Dh)),
        'nsa_k_norm': gain((L, N_NSA_BRANCHES, Dh)),
        'cmp_k_w1': w((L, CMP_BLOCK * Dh, Dh), CMP_BLOCK * Dh),
        'cmp_k_w2': w((L, Dh, Dh), Dh),
        'cmp_k_pos': 0.1 * jax.random.normal(next(keys), (L, CMP_BLOCK, Dh), f32),
        'cmp_v_w1': w((L, CMP_BLOCK * Dh, Dh), CMP_BLOCK * Dh),
        'cmp_v_w2': w((L, Dh, Dh), Dh),
        'cmp_v_pos': 0.1 * jax.random.normal(next(keys), (L, CMP_BLOCK, Dh), f32),
        'mla_q_a_norm': gain((L, MLA_Q_RANK)),
        'mla_w_q_up': w((L, MLA_Q_RANK, MLA_HEADS * MLA_QK_DIM), MLA_Q_RANK),
        'mla_kv_a_norm': gain((L, MLA_KV_RANK)),
        'mla_w_kv_up': w((L, MLA_KV_RANK, MLA_HEADS * (MLA_NOPE_DIM + MLA_V_DIM)), MLA_KV_RANK),
        'mla_q_norm': gain((L, MLA_QK_DIM)),
        'mla_k_norm': gain((L, MLA_QK_DIM)),
        'nsa_out_norm': gain((L, NSA_OUT)),
        'mla_out_norm': gain((L, MLA_OUT)),
        'w_out': w((L, MIX_WIDTH, D), MIX_WIDTH),
        'ffn_norm': gain((L, D)),
        'dense_w_gate': w((n_dense, D, D_FF), D),
        'dense_w_up': w((n_dense, D, D_FF), D),
        'dense_w_down': w((n_dense, D_FF, D), D_FF),
        'moe_router': w((n_moe, D, N_EXPERTS), D),
        'moe_w_gate': w((n_moe, N_EXPERTS, D, D_FF), D),
        'moe_w_up': w((n_moe, N_EXPERTS, D, D_FF), D),
        'moe_w_down': w((n_moe, N_EXPERTS, D_FF, D), D_FF),
    }


def reference(x, positions, attn_norm, w_in, nsa_q_norm, nsa_k_norm, cmp_k_w1, cmp_k_w2, cmp_k_pos,
              cmp_v_w1, cmp_v_w2, cmp_v_pos, mla_q_a_norm, mla_w_q_up, mla_kv_a_norm, mla_w_kv_up,
              mla_q_norm, mla_k_norm, nsa_out_norm, mla_out_norm, w_out, ffn_norm,
              dense_w_gate, dense_w_up, dense_w_down, moe_router, moe_w_gate, moe_w_up, moe_w_down):
    B, S, D = x.shape
    G, Hg, Dh = NSA_KV_GROUPS, NSA_HEADS_PER_GROUP, NSA_HEAD_DIM
    cos_n, sin_n = rope_tables(positions, NSA_ROT_DIM)
    cos_m, sin_m = rope_tables(positions, MLA_ROPE_DIM)
    nc = (S - CMP_BLOCK) // CMP_STRIDE + 1
    cmp_pos = positions[:, np.arange(nc) * CMP_STRIDE + CMP_BLOCK - 1]
    cos_c, sin_c = rope_tables(cmp_pos, NSA_ROT_DIM)

    for layer in range(DEPTH):
        h = rms_norm(x, attn_norm[layer])
        proj = h @ w_in[layer]
        q_n, kv_n, g_n, qa_m, kva_m = jnp.split(proj, SPLITS, axis=-1)

        q = apply_rope(rms_norm(q_n.reshape(B, S, G, Hg, Dh), nsa_q_norm[layer]), cos_n, sin_n)
        kv = kv_n.reshape(B, S, N_NSA_BRANCHES, 2, G, Dh)
        k_c = compress(kv[:, :, 0, 0], cmp_k_w1[layer], cmp_k_w2[layer], cmp_k_pos[layer])
        k_c = apply_rope(rms_norm(k_c, nsa_k_norm[layer, 0]), cos_c, sin_c)
        v_c = compress(kv[:, :, 0, 1], cmp_v_w1[layer], cmp_v_w2[layer], cmp_v_pos[layer])
        k_s = apply_rope(rms_norm(kv[:, :, 1, 0], nsa_k_norm[layer, 1]), cos_n, sin_n)
        v_s = kv[:, :, 1, 1]
        k_w = apply_rope(rms_norm(kv[:, :, 2, 0], nsa_k_norm[layer, 2]), cos_n, sin_n)
        v_w = kv[:, :, 2, 1]
        gates = jax.nn.sigmoid(g_n.astype(jnp.float32)).astype(x.dtype).reshape(B, S, G, Hg, N_NSA_BRANCHES)
        o_nsa = nsa_attention(q, k_c, v_c, k_s, v_s, k_w, v_w, gates)

        c_q = rms_norm(qa_m, mla_q_a_norm[layer])
        q_m = (c_q @ mla_w_q_up[layer]).reshape(B, S, MLA_HEADS, MLA_QK_DIM)
        c_kv = rms_norm(kva_m[..., :MLA_KV_RANK], mla_kv_a_norm[layer])
        k_r = kva_m[..., MLA_KV_RANK:]
        kv_m = (c_kv @ mla_w_kv_up[layer]).reshape(B, S, MLA_HEADS, MLA_NOPE_DIM + MLA_V_DIM)
        k_nope, v_m = kv_m[..., :MLA_NOPE_DIM], kv_m[..., MLA_NOPE_DIM:]
        k_m = jnp.concatenate([k_nope, jnp.broadcast_to(k_r[:, :, None, :], (B, S, MLA_HEADS, MLA_ROPE_DIM))], axis=-1)
        q_m = rms_norm(q_m, mla_q_norm[layer])
        k_m = rms_norm(k_m, mla_k_norm[layer])
        q_m = jnp.concatenate([q_m[..., :MLA_NOPE_DIM], apply_rope(q_m[..., MLA_NOPE_DIM:], cos_m, sin_m)], axis=-1)
        k_m = jnp.concatenate([k_m[..., :MLA_NOPE_DIM], apply_rope(k_m[..., MLA_NOPE_DIM:], cos_m, sin_m)], axis=-1)
        o_mla = causal_attention(q_m, k_m, v_m)

        mixed = jnp.concatenate([rms_norm(o_nsa, nsa_out_norm[layer]), rms_norm(o_mla, mla_out_norm[layer])], axis=-1)
        x = x + mixed @ w_out[layer]

        h = rms_norm(x, ffn_norm[layer])
        i = layer // 2
        if layer % 2 == 0:
            f = swiglu(h, dense_w_gate[i], dense_w_up[i], dense_w_down[i])
        else:
            f = moe_swiglu(h.reshape(B * S, D), moe_router[i], moe_w_gate[i], moe_w_up[i], moe_w_down[i]).reshape(B, S, D)
        x = x + f
    return x
```

```python
import functools

import jax
import jax.numpy as jnp
from jax import lax
from jax.experimental import pallas as pl
from jax.experimental.pallas import tpu as pltpu

F32 = jnp.float32
BF16 = jnp.bfloat16

LANES = 128
VMEM_CAP_BYTES = 56 * 1024 * 1024

NORM_EPS = 1e-6
ROPE_THETA = 500000.0
Q_BLOCK = 128

NSA_HEAD_DIM = 128
NSA_KV_GROUPS = 2
NSA_HEADS_PER_GROUP = 4
NSA_HEADS = NSA_KV_GROUPS * NSA_HEADS_PER_GROUP
NSA_ROT_DIM = NSA_HEAD_DIM // 4
N_NSA_BRANCHES = 3
CMP_BLOCK = 32
CMP_STRIDE = 16
SEL_BLOCK = 64
SEL_SHIFT = SEL_BLOCK.bit_length() - 1
assert 1 << SEL_SHIFT == SEL_BLOCK
N_SEL_BLOCKS = 16
WINDOW = 512
SEL_FORCE = 1.0e4

MLA_V_DIM = 128
MLA_HEADS = 8
MLA_Q_RANK = 512
MLA_KV_RANK = 512
MLA_NOPE_DIM = 128
MLA_ROPE_DIM = 64
MLA_QK_DIM = MLA_NOPE_DIM + MLA_ROPE_DIM
MLA_QK_PAD = 256

N_EXPERTS = 8
TOP_K = 2

NSA_Q_COLS = NSA_HEADS * NSA_HEAD_DIM
NSA_KV_COLS = N_NSA_BRANCHES * 2 * NSA_KV_GROUPS * NSA_HEAD_DIM
NSA_GATE_COLS = NSA_HEADS * N_NSA_BRANCHES
P_Q0 = 0
P_KV0 = P_Q0 + NSA_Q_COLS
P_QA0 = P_KV0 + NSA_KV_COLS
P_KVA0 = P_QA0 + MLA_Q_RANK
P_KR0 = P_KVA0 + MLA_KV_RANK
P_G0 = P_KR0 + LANES
P_COLS = P_G0 + LANES

TM = 512
TN_PROJ = 768
TN_OUT = 512
TF = 512
TS_PREP = 1024
TK_ATT = 512
TQ_MLA = 512
MOE_TM = 512
GATHER_ROWS = 256


def _cparams(sem, vmem_bytes):
    return pltpu.CompilerParams(dimension_semantics=sem,
                                vmem_limit_bytes=int(min(VMEM_CAP_BYTES, max(vmem_bytes, 16 * 1024 * 1024))))


def _rms(x, g):
    ms = jnp.mean(x * x, axis=-1, keepdims=True)
    return x * lax.rsqrt(ms + NORM_EPS) * g


def _rope_lanes(x, tab, half):
    return x * tab[0] + pltpu.roll(x, half, 1) * tab[1] + pltpu.roll(x, LANES - half, 1) * tab[2]


def _dot_nt(a, b):
    return lax.dot_general(a, b, (((1,), (1,)), ((), ())), preferred_element_type=F32)


def _rms_matmul_kernel(x_ref, g_ref, w_ref, o_ref, h_ref):
    @pl.when(pl.program_id(1) == 0)
    def _():
        h_ref[...] = _rms(x_ref[...], g_ref[...]).astype(BF16)

    o_ref[...] = jnp.dot(h_ref[...], w_ref[...], preferred_element_type=F32)


def rms_matmul(x, g, w, tm, tn):
    T, K = x.shape
    N = w.shape[1]
    vmem = 2 * tm * K * 4 + tm * K * 2 + 2 * K * tn * 2 + 2 * tm * tn * 4
    return pl.pallas_call(
        _rms_matmul_kernel,
        grid=(T // tm, N // tn),
        in_specs=[pl.BlockSpec((tm, K), lambda i, j: (i, 0)),
                  pl.BlockSpec((1, K), lambda i, j: (0, 0)),
                  pl.BlockSpec((K, tn), lambda i, j: (0, j))],
        out_specs=pl.BlockSpec((tm, tn), lambda i, j: (i, j)),
        out_shape=jax.ShapeDtypeStruct((T, N), F32),
        scratch_shapes=[pltpu.VMEM((tm, K), BF16)],
        compiler_params=_cparams(("parallel", "arbitrary"), 2 * vmem),
        name="rms_matmul",
    )(x, g, w)


def _nsa_prep_kernel(kc_in, vc_in, ks_in, vs_in, kw_in, vw_in, rope_ref, gk_ref,
                     kc_out, vc_out, ks_out, vs_out, kw_out, vw_out):
    tab = rope_ref[...]
    half = NSA_ROT_DIM // 2
    kc_out[0, 0] = kc_in[...]
    vc_out[0, 0] = vc_in[...]
    ks_out[0, 0] = _rope_lanes(_rms(ks_in[...], gk_ref[1:2, :]), tab, half).astype(BF16)
    kw_out[0, 0] = _rope_lanes(_rms(kw_in[...], gk_ref[2:3, :]), tab, half).astype(BF16)
    vs_out[0, 0] = vs_in[...].astype(BF16)
    vw_out[0, 0] = vw_in[...].astype(BF16)


def nsa_prep(proj, rope_n, gk, B, S):
    ts = min(TS_PREP, S)
    ns = S // ts
    G, Dh = NSA_KV_GROUPS, NSA_HEAD_DIM
    kv_blk0 = P_KV0 // Dh

    def in_spec(branch, kv):
        off = kv_blk0 + (branch * 2 + kv) * G
        return pl.BlockSpec((ts, Dh), lambda b, s, g: (b * ns + s, off + g))

    out_spec = pl.BlockSpec((1, 1, ts, Dh), lambda b, s, g: (b, g, s, 0))
    f32_out = jax.ShapeDtypeStruct((B, G, S, Dh), F32)
    bf_out = jax.ShapeDtypeStruct((B, G, S, Dh), BF16)
    return pl.pallas_call(
        _nsa_prep_kernel,
        grid=(B, ns, G),
        in_specs=[in_spec(0, 0), in_spec(0, 1), in_spec(1, 0), in_spec(1, 1), in_spec(2, 0), in_spec(2, 1),
                  pl.BlockSpec((3, ts, Dh), lambda b, s, g: (0, b * ns + s, 0)),
                  pl.BlockSpec((N_NSA_BRANCHES, Dh), lambda b, s, g: (0, 0))],
        out_specs=[out_spec] * 6,
        out_shape=[f32_out, f32_out, bf_out, bf_out, bf_out, bf_out],
        compiler_params=_cparams(("parallel", "parallel", "parallel"), 2 * 2 * 12 * ts * Dh * 4),
        name="nsa_prep",
    )(proj, proj, proj, proj, proj, proj, rope_n, gk)


def _compress_one(t_ref, w1_ref, w2_ref, pos_ref):
    t = t_ref[0, 0]
    ncp = t.shape[0]
    u = jnp.dot((t + pos_ref[0]).astype(BF16), w1_ref[0], preferred_element_type=F32)
    v = jnp.dot((t + pos_ref[1]).astype(BF16), w1_ref[1], preferred_element_type=F32)
    pre = u + pltpu.roll(v, ncp - 1, 0)
    return jnp.dot(jax.nn.gelu(pre).astype(BF16), w2_ref[...], preferred_element_type=F32)


def _compress_kernel(tk_ref, tv_ref, w1k_ref, w2k_ref, posk_ref, w1v_ref, w2v_ref, posv_ref, rope_ref, gk_ref,
                     kc_ref, vc_ref):
    k = _compress_one(tk_ref, w1k_ref, w2k_ref, posk_ref)
    kc_ref[0, 0] = _rope_lanes(_rms(k, gk_ref[0:1, :]), rope_ref[:, 0], NSA_ROT_DIM // 2).astype(BF16)
    vc_ref[0, 0] = _compress_one(tv_ref, w1v_ref, w2v_ref, posv_ref).astype(BF16)


def nsa_compress(kc_raw, vc_raw, w1k, w2k, posk, w1v, w2v, posv, rope_c, gk):
    B, G, S, Dh = kc_raw.shape
    ncp = S // CMP_STRIDE
    kw = CMP_STRIDE * Dh
    tk = kc_raw.reshape(B, G, ncp, kw)
    tv = vc_raw.reshape(B, G, ncp, kw)
    t_spec = pl.BlockSpec((1, 1, ncp, kw), lambda b, g: (b, g, 0, 0))
    w1_spec = pl.BlockSpec((2, kw, Dh), lambda b, g: (0, 0, 0))
    w2_spec = pl.BlockSpec((Dh, Dh), lambda b, g: (0, 0))
    pos_spec = pl.BlockSpec((2, 1, kw), lambda b, g: (0, 0, 0))
    out_spec = pl.BlockSpec((1, 1, ncp, Dh), lambda b, g: (b, g, 0, 0))
    out = jax.ShapeDtypeStruct((B, G, ncp, Dh), BF16)
    return pl.pallas_call(
        _compress_kernel,
        grid=(B, G),
        in_specs=[t_spec, t_spec, w1_spec, w2_spec, pos_spec, w1_spec, w2_spec, pos_spec,
                  pl.BlockSpec((3, 1, ncp, Dh), lambda b, g: (0, b, 0, 0)),
                  pl.BlockSpec((N_NSA_BRANCHES, Dh), lambda b, g: (0, 0))],
        out_specs=[out_spec, out_spec],
        out_shape=[out, out],
        compiler_params=_cparams(("parallel", "parallel"), 2 * (4 * ncp * kw * 4 + 8 * kw * Dh * 2)),
        name="nsa_compress",
    )(tk, tv, w1k, w2k, posk, w1v, w2v, posv, rope_c, gk)


def _softmax_masked(s, mask):
    sm = jnp.where(mask, s, -1e30)
    m = jnp.max(sm, axis=-1, keepdims=True)
    e = jnp.where(mask, jnp.exp(sm - m), 0.0)
    l = jnp.sum(e, axis=-1, keepdims=True)
    return e / jnp.where(l > 0.0, l, 1.0)


def _nsa_attn_kernel(q_ref, gl_ref, rope_ref, qg_ref, kc_ref, vc_ref, ks_ref, vs_ref, kw_ref, vw_ref, o_ref,
                     *, seq, n_sel):
    Hg, Dh, QB = NSA_HEADS_PER_GROUP, NSA_HEAD_DIM, Q_BLOCK
    R = Hg * QB
    g = pl.program_id(1)
    i = pl.program_id(2)
    s0 = i * QB
    ncp = kc_ref.shape[2]
    nb = max(LANES, seq // SEL_BLOCK)
    scale = Dh ** -0.5

    qt = q_ref[...]
    rows = jnp.concatenate([qt[:, h * Dh:(h + 1) * Dh] for h in range(Hg)], axis=0)
    tab = rope_ref[...]
    tab4 = [jnp.concatenate([tab[k]] * Hg, axis=0) for k in range(3)]
    qb = _rope_lanes(_rms(rows, qg_ref[...]), tab4, NSA_ROT_DIM // 2).astype(BF16)
    t_q = s0 + lax.broadcasted_iota(jnp.int32, (QB, 1), 0)

    s_c = (_dot_nt(qb, kc_ref[0, 0]) * scale).reshape(Hg, QB, ncp)
    cmp_end = lax.broadcasted_iota(jnp.int32, (1, ncp), 1) * CMP_STRIDE + (CMP_BLOCK - 1)
    p_c = _softmax_masked(s_c, (cmp_end <= t_q)[None])
    o_c = jnp.dot(p_c.reshape(R, ncp).astype(BF16), vc_ref[0, 0], preferred_element_type=F32)

    c_i = lax.broadcasted_iota(jnp.int32, (ncp, nb), 0) * CMP_STRIDE
    j_i = lax.broadcasted_iota(jnp.int32, (ncp, nb), 1) * SEL_BLOCK
    overlap = jnp.where((c_i < j_i + SEL_BLOCK) & (c_i + (CMP_BLOCK - 1) >= j_i), 1.0, 0.0).astype(BF16)
    imp = jnp.dot(jnp.sum(p_c, axis=0).astype(BF16), overlap, preferred_element_type=F32)
    jq = lax.broadcasted_iota(jnp.int32, (QB, nb), 1)
    cur = jnp.right_shift(t_q, SEL_SHIFT)
    forced = (jq == 0) | (jq == cur) | (jq == cur - 1)
    imp = jnp.where(jq <= cur, imp + jnp.where(forced, SEL_FORCE, 0.0), -jnp.inf)
    jf = jq.astype(F32)
    sel = jnp.zeros((QB, nb), F32)
    for _ in range(n_sel):
        mx = jnp.max(imp, axis=-1, keepdims=True)
        first = jnp.min(jnp.where(imp == mx, jf, float(nb)), axis=-1, keepdims=True)
        pick = (jf == first) & (mx > -jnp.inf)
        sel = jnp.where(pick, 1.0, sel)
        imp = jnp.where(pick, -jnp.inf, imp)
    sel_b = sel.astype(BF16)

    tk = TK_ATT
    jb = lax.broadcasted_iota(jnp.int32, (nb, tk), 0)
    kl = lax.broadcasted_iota(jnp.int32, (nb, tk), 1)
    kl_row = lax.broadcasted_iota(jnp.int32, (1, tk), 1)

    def sel_step(kt, carry):
        m, l, acc = carry
        k0 = pl.multiple_of(kt * tk, tk)
        s = (_dot_nt(qb, ks_ref[0, 0, pl.ds(k0, tk), :]) * scale).reshape(Hg, QB, tk)
        expand = jnp.where(jb == jnp.right_shift(k0 + kl, SEL_SHIFT), 1.0, 0.0).astype(BF16)
        chosen = jnp.dot(sel_b, expand, preferred_element_type=F32)
        mask = ((chosen > 0.5) & (k0 + kl_row <= t_q))[None]
        sm = jnp.where(mask, s, -1e30)
        m_new = jnp.maximum(m, jnp.max(sm, axis=-1, keepdims=True))
        alpha = jnp.exp(m - m_new)
        p = jnp.where(mask, jnp.exp(sm - m_new), 0.0)
        l = alpha * l + jnp.sum(p, axis=-1, keepdims=True)
        pv = jnp.dot(p.reshape(R, tk).astype(BF16), vs_ref[0, 0, pl.ds(k0, tk), :], preferred_element_type=F32)
        return m_new, l, alpha * acc + pv.reshape(Hg, QB, Dh)

    n_tiles = (s0 + QB + tk - 1) // tk
    init = (jnp.full((Hg, QB, 1), -1e30, F32), jnp.zeros((Hg, QB, 1), F32), jnp.zeros((Hg, QB, Dh), F32))
    _, l_s, acc_s = lax.fori_loop(0, n_tiles, sel_step, init)
    o_s = (acc_s / l_s).reshape(R, Dh)

    wk = WINDOW + QB
    w0 = pl.multiple_of(jnp.maximum(s0 - WINDOW, 0), QB)
    s_w = (_dot_nt(qb, kw_ref[0, 0, pl.ds(w0, wk), :]) * scale).reshape(Hg, QB, wk)
    diff = t_q - (w0 + lax.broadcasted_iota(jnp.int32, (1, wk), 1))
    p_w = _softmax_masked(s_w, ((diff >= 0) & (diff < WINDOW))[None])
    o_w = jnp.dot(p_w.reshape(R, wk).astype(BF16), vw_ref[0, 0, pl.ds(w0, wk), :], preferred_element_type=F32)

    gates = jax.nn.sigmoid(gl_ref[...])
    gates = jnp.where(g == 0, gates, pltpu.roll(gates, LANES - Hg * N_NSA_BRANCHES, 1))
    for h in range(Hg):
        c0 = h * N_NSA_BRANCHES
        rs = slice(h * QB, (h + 1) * QB)
        o_ref[:, h * Dh:(h + 1) * Dh] = (gates[:, c0:c0 + 1] * o_c[rs] + gates[:, c0 + 1:c0 + 2] * o_s[rs]
                                         + gates[:, c0 + 2:c0 + 3] * o_w[rs])


def nsa_attention(proj, rope_n, qg, kc, vc, ks, vs, kw, vw, B, S):
    G, Hg, Dh, QB = NSA_KV_GROUPS, NSA_HEADS_PER_GROUP, NSA_HEAD_DIM, Q_BLOCK
    nq = S // QB
    ncp = kc.shape[2]
    n_sel = min(N_SEL_BLOCKS, S // SEL_BLOCK)
    qcols = Hg * Dh
    kv_spec = pl.BlockSpec((1, 1, S, Dh), lambda b, g, i: (b, g, 0, 0))
    c_spec = pl.BlockSpec((1, 1, ncp, Dh), lambda b, g, i: (b, g, 0, 0))
    vmem = 2 * (4 * S * Dh * 2 + 2 * ncp * Dh * 2) + 24 * Hg * QB * max(TK_ATT, ncp, WINDOW + QB) * 4
    return pl.pallas_call(
        functools.partial(_nsa_attn_kernel, seq=S, n_sel=n_sel),
        grid=(B, G, nq),
        in_specs=[pl.BlockSpec((QB, qcols), lambda b, g, i: (b * nq + i, P_Q0 // qcols + g)),
                  pl.BlockSpec((QB, LANES), lambda b, g, i: (b * nq + i, P_G0 // LANES)),
                  pl.BlockSpec((3, QB, Dh), lambda b, g, i: (0, b * nq + i, 0)),
                  pl.BlockSpec((1, Dh), lambda b, g, i: (0, 0)),
                  c_spec, c_spec, kv_spec, kv_spec, kv_spec, kv_spec],
        out_specs=pl.BlockSpec((QB, qcols), lambda b, g, i: (b * nq + i, g)),
        out_shape=jax.ShapeDtypeStruct((B * S, G * qcols), F32),
        compiler_params=_cparams(("parallel", "parallel", "arbitrary"), vmem),
        name="nsa_attention",
    )(proj, proj, rope_n, qg, kc, vc, ks, vs, kw, vw)


def _mla_q_kernel(qa_ref, ga_ref, w_ref, gq_ref, rope_ref, o_ref, c_ref):
    @pl.when(pl.program_id(1) == 0)
    def _():
        c_ref[...] = _rms(qa_ref[...], ga_ref[...]).astype(BF16)

    q = jnp.dot(c_ref[...], w_ref[...], preferred_element_type=F32)
    ms = jnp.sum(q * q, axis=-1, keepdims=True) * (1.0 / MLA_QK_DIM)
    qn = q * lax.rsqrt(ms + NORM_EPS) * gq_ref[...]
    o_ref[0, 0, :, :LANES] = qn[:, :LANES].astype(BF16)
    o_ref[0, 0, :, LANES:] = _rope_lanes(qn[:, LANES:], rope_ref[...], MLA_ROPE_DIM // 2).astype(BF16)


def mla_q_prep(proj, ga, wq, gq, rope_m, B, S):
    tm = min(TM, S)
    ns = S // tm
    H, R, W = MLA_HEADS, MLA_Q_RANK, MLA_QK_PAD
    return pl.pallas_call(
        _mla_q_kernel,
        grid=(B * ns, H),
        in_specs=[pl.BlockSpec((tm, R), lambda r, h: (r, P_QA0 // R)),
                  pl.BlockSpec((1, R), lambda r, h: (0, 0)),
                  pl.BlockSpec((R, W), lambda r, h: (0, h)),
                  pl.BlockSpec((1, W), lambda r, h: (0, 0)),
                  pl.BlockSpec((3, tm, LANES), lambda r, h: (0, r, 0))],
        out_specs=pl.BlockSpec((1, 1, tm, W), lambda r, h: (r // ns, h, r % ns, 0)),
        out_shape=jax.ShapeDtypeStruct((B, H, S, W), BF16),
        scratch_shapes=[pltpu.VMEM((tm, R), BF16)],
        compiler_params=_cparams(("parallel", "arbitrary"), 4 * (tm * R * 4 + R * W * 2 + tm * W * 6)),
        name="mla_q_prep",
    )(proj, ga, wq, gq, rope_m)


def _mla_kv_kernel(kva_ref, kr_ref, ga_ref, w_ref, gk_ref, rope_ref, k_ref, v_ref, c_ref):
    @pl.when(pl.program_id(1) == 0)
    def _():
        c_ref[...] = _rms(kva_ref[...], ga_ref[...]).astype(BF16)

    kv = jnp.dot(c_ref[...], w_ref[...], preferred_element_type=F32)
    k_nope = kv[:, :LANES]
    k_rot = kr_ref[...]
    ms = (jnp.sum(k_nope * k_nope, axis=-1, keepdims=True)
          + jnp.sum(k_rot * k_rot, axis=-1, keepdims=True)) * (1.0 / MLA_QK_DIM)
    r = lax.rsqrt(ms + NORM_EPS)
    k_ref[0, 0, :, :LANES] = (k_nope * r * gk_ref[:, :LANES]).astype(BF16)
    k_ref[0, 0, :, LANES:] = _rope_lanes(k_rot * r * gk_ref[:, LANES:], rope_ref[...], MLA_ROPE_DIM // 2).astype(BF16)
    v_ref[0, 0] = kv[:, LANES:].astype(BF16)


def mla_kv_prep(proj, ga, wkv, gk, rope_m, B, S):
    tm = min(TM, S)
    ns = S // tm
    H, R, W = MLA_HEADS, MLA_KV_RANK, MLA_QK_PAD
    return pl.pallas_call(
        _mla_kv_kernel,
        grid=(B * ns, H),
        in_specs=[pl.BlockSpec((tm, R), lambda r, h: (r, P_KVA0 // R)),
                  pl.BlockSpec((tm, LANES), lambda r, h: (r, P_KR0 // LANES)),
                  pl.BlockSpec((1, R), lambda r, h: (0, 0)),
                  pl.BlockSpec((R, W), lambda r, h: (0, h)),
                  pl.BlockSpec((1, W), lambda r, h: (0, 0)),
                  pl.BlockSpec((3, tm, LANES), lambda r, h: (0, r, 0))],
        out_specs=[pl.BlockSpec((1, 1, tm, W), lambda r, h: (r // ns, h, r % ns, 0)),
                   pl.BlockSpec((1, 1, tm, MLA_V_DIM), lambda r, h: (r // ns, h, r % ns, 0))],
        out_shape=[jax.ShapeDtypeStruct((B, H, S, W), BF16), jax.ShapeDtypeStruct((B, H, S, MLA_V_DIM), BF16)],
        scratch_shapes=[pltpu.VMEM((tm, R), BF16)],
        compiler_params=_cparams(("parallel", "arbitrary"), 4 * (tm * R * 4 + R * W * 2 + tm * W * 8)),
        name="mla_kv_prep",
    )(proj, proj, ga, wkv, gk, rope_m)


def _mla_attn_kernel(q_ref, k_ref, v_ref, o_ref, *, tq, tk):
    s0 = pl.program_id(2) * tq
    q = q_ref[0, 0]
    scale = MLA_QK_DIM ** -0.5
    t_q = s0 + lax.broadcasted_iota(jnp.int32, (tq, 1), 0)
    kl_row = lax.broadcasted_iota(jnp.int32, (1, tk), 1)

    def step(kt, carry):
        m, l, acc = carry
        k0 = pl.multiple_of(kt * tk, tk)
        s = _dot_nt(q, k_ref[0, 0, pl.ds(k0, tk), :]) * scale
        mask = k0 + kl_row <= t_q
        sm = jnp.where(mask, s, -1e30)
        m_new = jnp.maximum(m, jnp.max(sm, axis=-1, keepdims=True))
        alpha = jnp.exp(m - m_new)
        p = jnp.where(mask, jnp.exp(sm - m_new), 0.0)
        l = alpha * l + jnp.sum(p, axis=-1, keepdims=True)
        pv = jnp.dot(p.astype(BF16), v_ref[0, 0, pl.ds(k0, tk), :], preferred_element_type=F32)
        return m_new, l, alpha * acc + pv

    n_tiles = (s0 + tq + tk - 1) // tk
    init = (jnp.full((tq, 1), -1e30, F32), jnp.zeros((tq, 1), F32), jnp.zeros((tq, MLA_V_DIM), F32))
    _, l, acc = lax.fori_loop(0, n_tiles, step, init)
    o_ref[...] = acc / l


def mla_attention(q, k, v):
    B, H, S, W = q.shape
    tq = min(TQ_MLA, S)
    tk = min(TK_ATT, S)
    nq = S // tq
    Dv = v.shape[-1]
    vmem = 2 * (S * W * 2 + S * Dv * 2) + 4 * tq * W * 2 + 24 * tq * tk * 4
    return pl.pallas_call(
        functools.partial(_mla_attn_kernel, tq=tq, tk=tk),
        grid=(B, H, nq),
        in_specs=[pl.BlockSpec((1, 1, tq, W), lambda b, h, i: (b, h, i, 0)),
                  pl.BlockSpec((1, 1, S, W), lambda b, h, i: (b, h, 0, 0)),
                  pl.BlockSpec((1, 1, S, Dv), lambda b, h, i: (b, h, 0, 0))],
        out_specs=pl.BlockSpec((tq, Dv), lambda b, h, i: (b * nq + i, h)),
        out_shape=jax.ShapeDtypeStruct((B * S, H * Dv), F32),
        compiler_params=_cparams(("parallel", "parallel", "arbitrary"), vmem),
        name="mla_attention",
    )(q, k, v)


def _out_proj_kernel(x_ref, a_ref, b_ref, ga_ref, gb_ref, wa_ref, wb_ref, o_ref, na_ref, nb_ref):
    @pl.when(pl.program_id(1) == 0)
    def _():
        na_ref[...] = _rms(a_ref[...], ga_ref[...]).astype(BF16)
        nb_ref[...] = _rms(b_ref[...], gb_ref[...]).astype(BF16)

    o_ref[...] = (x_ref[...] + jnp.dot(na_ref[...], wa_ref[...], preferred_element_type=F32)
                  + jnp.dot(nb_ref[...], wb_ref[...], preferred_element_type=F32))


def out_proj(x, o_nsa, o_mla, g_nsa, g_mla, w_out):
    T, D = x.shape
    Ka, Kb = o_nsa.shape[1], o_mla.shape[1]
    assert Ka == Kb
    tm, tn = min(TM, T), TN_OUT
    vmem = 2 * (tm * tn * 8 + 2 * tm * Ka * 4 + 2 * Ka * tn * 2) + 2 * tm * Ka * 2
    return pl.pallas_call(
        _out_proj_kernel,
        grid=(T // tm, D // tn),
        in_specs=[pl.BlockSpec((tm, tn), lambda i, j: (i, j)),
                  pl.BlockSpec((tm, Ka), lambda i, j: (i, 0)),
                  pl.BlockSpec((tm, Kb), lambda i, j: (i, 0)),
                  pl.BlockSpec((1, Ka), lambda i, j: (0, 0)),
                  pl.BlockSpec((1, Kb), lambda i, j: (0, 0)),
                  pl.BlockSpec((Ka, tn), lambda i, j: (0, j)),
                  pl.BlockSpec((Kb, tn), lambda i, j: (1, j))],
        out_specs=pl.BlockSpec((tm, tn), lambda i, j: (i, j)),
        out_shape=jax.ShapeDtypeStruct((T, D), F32),
        scratch_shapes=[pltpu.VMEM((tm, Ka), BF16), pltpu.VMEM((tm, Kb), BF16)],
        compiler_params=_cparams(("parallel", "arbitrary"), 2 * vmem),
        name="out_proj",
    )(x, o_nsa, o_mla, g_nsa, g_mla, w_out, w_out)


def _swiglu_kernel(be_ref, x_ref, g_ref, wg_ref, wu_ref, wd_ref, o_ref, h_ref, acc_ref, *, residual):
    i = pl.program_id(0)
    f = pl.program_id(1)
    live = i < be_ref[0]

    @pl.when(live & (f == 0))
    def _():
        h_ref[...] = _rms(x_ref[...], g_ref[...]).astype(BF16)
        acc_ref[...] = jnp.zeros_like(acc_ref)

    @pl.when(live)
    def _():
        h = h_ref[...]
        a = jax.nn.silu(jnp.dot(h, wg_ref[0], preferred_element_type=F32)) * jnp.dot(
            h, wu_ref[0], preferred_element_type=F32)
        acc_ref[...] += jnp.dot(a.astype(BF16), wd_ref[0], preferred_element_type=F32)

    last = f == pl.num_programs(1) - 1

    @pl.when(live & last)
    def _():
        o_ref[...] = x_ref[...] + acc_ref[...] if residual else acc_ref[...]

    @pl.when(jnp.logical_not(live) & last)
    def _():
        o_ref[...] = jnp.zeros_like(o_ref)


def swiglu_blocks(x, g, wg, wu, wd, block_info, tm, residual):
    R, D = x.shape
    F = wg.shape[2]
    tf = TF
    nf = F // tf

    def wcol(i, f, be):
        live = i < be[0]
        return (be[1 + i], 0, jnp.where(live, f, nf - 1))

    def wrow(i, f, be):
        live = i < be[0]
        return (be[1 + i], jnp.where(live, f, nf - 1), 0)

    vmem = 4 * tm * D * 4 + tm * D * 2 + tm * D * 4 + 2 * 3 * D * tf * 2 + 6 * tm * tf * 4
    grid_spec = pltpu.PrefetchScalarGridSpec(
        num_scalar_prefetch=1,
        grid=(R // tm, nf),
        in_specs=[pl.BlockSpec((tm, D), lambda i, f, be: (i, 0)),
                  pl.BlockSpec((1, D), lambda i, f, be: (0, 0)),
                  pl.BlockSpec((1, D, tf), wcol),
                  pl.BlockSpec((1, D, tf), wcol),
                  pl.BlockSpec((1, tf, D), wrow)],
        out_specs=pl.BlockSpec((tm, D), lambda i, f, be: (i, 0)),
        scratch_shapes=[pltpu.VMEM((tm, D), BF16), pltpu.VMEM((tm, D), F32)],
    )
    return pl.pallas_call(
        functools.partial(_swiglu_kernel, residual=residual),
        grid_spec=grid_spec,
        out_shape=jax.ShapeDtypeStruct((R, D), F32),
        compiler_params=_cparams(("parallel", "arbitrary"), vmem + (4 << 20)),
        name="swiglu",
    )(block_info, x, g, wg, wu, wd)


def _router_kernel(x_ref, g_ref, w_ref, o_ref):
    h = _rms(x_ref[...], g_ref[...]).astype(BF16)
    logits = jnp.dot(h, w_ref[...], preferred_element_type=F32)
    lane = lax.broadcasted_iota(jnp.int32, logits.shape, 1)
    lf = lane.astype(F32)
    lg = jnp.where(lane < N_EXPERTS, logits, -jnp.inf)
    m1 = jnp.max(lg, axis=-1, keepdims=True)
    i1 = jnp.min(jnp.where(lg == m1, lf, float(LANES)), axis=-1, keepdims=True)
    lg2 = jnp.where(lf == i1, -jnp.inf, lg)
    m2 = jnp.max(lg2, axis=-1, keepdims=True)
    i2 = jnp.min(jnp.where(lg2 == m2, lf, float(LANES)), axis=-1, keepdims=True)
    e2 = jnp.exp(m2 - m1)
    den = 1.0 + e2
    o_ref[...] = jnp.where(lane == 0, i1, jnp.where(lane == 1, i2, jnp.where(lane == 2, 1.0 / den, e2 / den)))


def moe_router(x, g, w_router_p):
    T, D = x.shape
    tm = min(TM, T)
    return pl.pallas_call(
        _router_kernel,
        grid=(T // tm,),
        in_specs=[pl.BlockSpec((tm, D), lambda i: (i, 0)),
                  pl.BlockSpec((1, D), lambda i: (0, 0)),
                  pl.BlockSpec((D, LANES), lambda i: (0, 0))],
        out_specs=pl.BlockSpec((tm, LANES), lambda i: (i, 0)),
        out_shape=jax.ShapeDtypeStruct((T, LANES), F32),
        compiler_params=_cparams(("parallel",), 4 * tm * D * 4),
        name="moe_router",
    )(x, g, w_router_p)


def _gather_kernel(idx_ref, src_ref, dst_ref, sem):
    base = pl.program_id(0) * GATHER_ROWS

    def row_copy(r):
        return pltpu.make_async_copy(src_ref.at[pl.ds(idx_ref[base + r], 1)], dst_ref.at[pl.ds(base + r, 1)], sem)

    def start(r, c):
        row_copy(r).start()
        return c

    def wait(r, c):
        row_copy(r).wait()
        return c

    lax.fori_loop(0, GATHER_ROWS, start, 0)
    lax.fori_loop(0, GATHER_ROWS, wait, 0)


def gather_rows(src, idx):
    n = idx.shape[0]
    D = src.shape[1]
    grid_spec = pltpu.PrefetchScalarGridSpec(
        num_scalar_prefetch=1,
        grid=(n // GATHER_ROWS,),
        in_specs=[pl.BlockSpec(memory_space=pl.ANY)],
        out_specs=pl.BlockSpec(memory_space=pl.ANY),
        scratch_shapes=[pltpu.SemaphoreType.DMA(())],
    )
    return pl.pallas_call(
        _gather_kernel,
        grid_spec=grid_spec,
        out_shape=jax.ShapeDtypeStruct((n, D), src.dtype),
        compiler_params=pltpu.CompilerParams(dimension_semantics=("arbitrary",)),
        name="gather_rows",
    )(idx, src)


def _combine_kernel(x_ref, ya_ref, yb_ref, gate_ref, o_ref):
    gate = gate_ref[...]
    o_ref[...] = x_ref[...] + (ya_ref[...] * gate[:, 2:3] + yb_ref[...] * gate[:, 3:4])


def moe_combine(x, y_pair, route):
    T, D = x.shape
    tm = min(TM, T)
    nt = T // tm
    return pl.pallas_call(
        _combine_kernel,
        grid=(nt,),
        in_specs=[pl.BlockSpec((tm, D), lambda i: (i, 0)),
                  pl.BlockSpec((tm, D), lambda i: (i, 0)),
                  pl.BlockSpec((tm, D), lambda i: (i + nt, 0)),
                  pl.BlockSpec((tm, LANES), lambda i: (i, 0))],
        out_specs=pl.BlockSpec((tm, D), lambda i: (i, 0)),
        out_shape=jax.ShapeDtypeStruct((T, D), F32),
        compiler_params=_cparams(("parallel",), 10 * tm * D * 4),
        name="moe_combine",
    )(x, y_pair, y_pair, route)


def moe_ffn(x, g, w_router_p, wg, wu, wd):
    T, D = x.shape
    n_assign = T * TOP_K
    route = moe_router(x, g, w_router_p)
    flat_e = route[:, :TOP_K].astype(jnp.int32).reshape(-1)
    order = jnp.argsort(flat_e)
    se = flat_e[order]
    stok = (order // TOP_K).astype(jnp.int32)
    counts = jnp.bincount(flat_e, length=N_EXPERTS)
    padded = (counts + MOE_TM - 1) // MOE_TM * MOE_TM
    pad_end = jnp.cumsum(padded)
    pad_start = pad_end - padded
    start = jnp.cumsum(counts) - counts
    dest = (pad_start[se] + jnp.arange(n_assign) - start[se]).astype(jnp.int32)
    n_blocks = -(-(n_assign + N_EXPERTS * (MOE_TM - 1)) // MOE_TM)
    n_rows = n_blocks * MOE_TM
    row_tok = jnp.zeros((n_rows,), jnp.int32).at[dest].set(stok)
    block_e = jnp.minimum(jnp.searchsorted(pad_end, jnp.arange(n_blocks) * MOE_TM, side='right'), N_EXPERTS - 1)
    n_live = (pad_end[-1] // MOE_TM).astype(jnp.int32)
    block_info = jnp.concatenate([n_live[None], block_e.astype(jnp.int32)])
    dest_flat = jnp.zeros((n_assign,), jnp.int32).at[order].set(dest)
    pair_idx = jnp.concatenate([dest_flat[0::2], dest_flat[1::2]])

    x_rows = gather_rows(x, row_tok)
    y_rows = swiglu_blocks(x_rows, g, wg, wu, wd, block_info, MOE_TM, residual=False)
    y_pair = gather_rows(y_rows, pair_idx)
    return moe_combine(x, y_pair, route)


def _rope_tab(pos, rot_dim):
    half = rot_dim // 2
    inv_freq = jnp.power(jnp.float32(ROPE_THETA), -jnp.arange(0, rot_dim, 2, dtype=F32) / rot_dim)
    ang = pos.astype(F32)[..., None] * inv_freq
    cos, sin = jnp.cos(ang), jnp.sin(ang)
    rest = LANES - rot_dim
    one = jnp.ones(pos.shape + (rest,), F32)
    zr = jnp.zeros(pos.shape + (rest,), F32)
    zh = jnp.zeros_like(sin)
    return jnp.stack([jnp.concatenate([cos, cos, one], -1),
                      jnp.concatenate([zh, sin, zr], -1),
                      jnp.concatenate([-sin, zh, zr], -1)])


def _pad_cols(w, n):
    return jnp.pad(w, ((0, 0), (0, n - w.shape[1])))


def kernel(x, positions, attn_norm, w_in, nsa_q_norm, nsa_k_norm, cmp_k_w1, cmp_k_w2, cmp_k_pos, cmp_v_w1, cmp_v_w2, cmp_v_pos, mla_q_a_norm, mla_w_q_up, mla_kv_a_norm, mla_w_kv_up, mla_q_norm, mla_k_norm, nsa_out_norm, mla_out_norm, w_out, ffn_norm, dense_w_gate, dense_w_up, dense_w_down, moe_router, moe_w_gate, moe_w_up, moe_w_down):
    B, S, D = x.shape
    T = B * S
    depth = w_in.shape[0]
    Dh = NSA_HEAD_DIM
    ncp = S // CMP_STRIDE
    kw = CMP_STRIDE * Dh

    rope_n = _rope_tab(positions, NSA_ROT_DIM).reshape(3, T, LANES)
    rope_m = _rope_tab(positions, MLA_ROPE_DIM).reshape(3, T, LANES)
    cmp_idx = jnp.minimum(jnp.arange(ncp) * CMP_STRIDE + CMP_BLOCK - 1, S - 1)
    rope_c = _rope_tab(positions[:, cmp_idx], NSA_ROT_DIM)

    g0 = NSA_Q_COLS + NSA_KV_COLS
    m0 = g0 + NSA_GATE_COLS
    xt = x.reshape(T, D)
    for layer in range(depth):
        w = w_in[layer]
        w_in_p = jnp.concatenate([w[:, :g0], w[:, m0:m0 + MLA_Q_RANK], w[:, m0 + MLA_Q_RANK:m0 + MLA_Q_RANK + MLA_KV_RANK],
                                  _pad_cols(w[:, m0 + MLA_Q_RANK + MLA_KV_RANK:], LANES),
                                  _pad_cols(w[:, g0:m0], LANES)], axis=1).astype(BF16)
        proj = rms_matmul(xt, attn_norm[layer][None], w_in_p, min(TM, T), TN_PROJ)

        gk = nsa_k_norm[layer]
        kc_raw, vc_raw, ks, vs, kwin, vwin = nsa_prep(proj, rope_n, gk, B, S)
        kc, vc = nsa_compress(
            kc_raw, vc_raw,
            cmp_k_w1[layer].reshape(2, kw, Dh).astype(BF16), cmp_k_w2[layer].astype(BF16),
            cmp_k_pos[layer].reshape(2, 1, kw),
            cmp_v_w1[layer].reshape(2, kw, Dh).astype(BF16), cmp_v_w2[layer].astype(BF16),
            cmp_v_pos[layer].reshape(2, 1, kw), rope_c, gk)
        o_nsa = nsa_attention(proj, rope_n, nsa_q_norm[layer][None], kc, vc, ks, vs, kwin, vwin, B, S)

        wq = jnp.pad(mla_w_q_up[layer].reshape(MLA_Q_RANK, MLA_HEADS, MLA_QK_DIM),
                     ((0, 0), (0, 0), (0, MLA_QK_PAD - MLA_QK_DIM))).reshape(MLA_Q_RANK, MLA_HEADS * MLA_QK_PAD)
        gq = jnp.pad(mla_q_norm[layer], (0, MLA_QK_PAD - MLA_QK_DIM))[None]
        gkm = jnp.pad(mla_k_norm[layer], (0, MLA_QK_PAD - MLA_QK_DIM))[None]
        q_m = mla_q_prep(proj, mla_q_a_norm[layer][None], wq.astype(BF16), gq, rope_m, B, S)
        k_m, v_m = mla_kv_prep(proj, mla_kv_a_norm[layer][None], mla_w_kv_up[layer].astype(BF16), gkm, rope_m, B, S)
        o_mla = mla_attention(q_m, k_m, v_m)

        xt = out_proj(xt, o_nsa, o_mla, nsa_out_norm[layer][None], mla_out_norm[layer][None],
                      w_out[layer].astype(BF16))

        i = layer // 2
        gf = ffn_norm[layer][None]
        if layer % 2 == 0:
            tm = min(TM, T)
            info = jnp.concatenate([jnp.full((1,), T // tm, jnp.int32), jnp.zeros((T // tm,), jnp.int32)])
            xt = swiglu_blocks(xt, gf, dense_w_gate[i][None].astype(BF16), dense_w_up[i][None].astype(BF16),
                               dense_w_down[i][None].astype(BF16), info, tm, residual=True)
        else:
            xt = moe_ffn(xt, gf, _pad_cols(moe_router[i], LANES).astype(BF16), moe_w_gate[i].astype(BF16),
                         moe_w_up[i].astype(BF16), moe_w_down[i].astype(BF16))
    return xt.reshape(B, S, D)
```

```python
import functools

import jax
import jax.numpy as jnp
from jax import lax
from jax.experimental import pallas as pl
from jax.experimental.pallas import tpu as pltpu

F32 = jnp.float32
BF16 = jnp.bfloat16

LANES = 128
VMEM_CAP_BYTES = 56 * 1024 * 1024

LOG2E = 1.4426950408889634
NEG = -1e30

NORM_EPS = 1e-6
ROPE_THETA = 500000.0
Q_BLOCK = 128

NSA_HEAD_DIM = 128
NSA_KV_GROUPS = 2
NSA_HEADS_PER_GROUP = 4
NSA_HEADS = NSA_KV_GROUPS * NSA_HEADS_PER_GROUP
NSA_ROT_DIM = NSA_HEAD_DIM // 4
N_NSA_BRANCHES = 3
CMP_BLOCK = 32
CMP_STRIDE = 16
SEL_BLOCK = 64
SEL_SHIFT = SEL_BLOCK.bit_length() - 1
assert 1 << SEL_SHIFT == SEL_BLOCK
N_SEL_BLOCKS = 16
WINDOW = 512
SEL_FORCE = 1.0e4

MLA_V_DIM = 128
MLA_HEADS = 8
MLA_Q_RANK = 512
MLA_KV_RANK = 512
MLA_NOPE_DIM = 128
MLA_ROPE_DIM = 64
MLA_QK_DIM = MLA_NOPE_DIM + MLA_ROPE_DIM
MLA_QK_PAD = 256

N_EXPERTS = 8
TOP_K = 2

NSA_Q_COLS = NSA_HEADS * NSA_HEAD_DIM
NSA_KV_COLS = N_NSA_BRANCHES * 2 * NSA_KV_GROUPS * NSA_HEAD_DIM
NSA_GATE_COLS = NSA_HEADS * N_NSA_BRANCHES
P_Q0 = 0
P_KV0 = P_Q0 + NSA_Q_COLS
P_QA0 = P_KV0 + NSA_KV_COLS
P_KVA0 = P_QA0 + MLA_Q_RANK
P_KR0 = P_KVA0 + MLA_KV_RANK
P_G0 = P_KR0 + LANES
P_COLS = P_G0 + LANES

TM = 512
TN_PROJ = 768
TN_OUT = 512
TF = 512
TS_PREP = 1024
TK_ATT = 512
TQ_MLA = 512
MLA_HEADS_PER_STEP = 2
MOE_TM = 512
GATHER_ROWS = 256


def _cparams(sem, vmem_bytes):
    return pltpu.CompilerParams(dimension_semantics=sem,
                                vmem_limit_bytes=int(min(VMEM_CAP_BYTES, max(vmem_bytes, 16 * 1024 * 1024))))


def _rms(x, g):
    ms = jnp.mean(x * x, axis=-1, keepdims=True)
    return x * lax.rsqrt(ms + NORM_EPS) * g


def _rope_lanes(x, tab, half):
    return x * tab[0] + pltpu.roll(x, half, 1) * tab[1] + pltpu.roll(x, LANES - half, 1) * tab[2]


def _dot_nt(a, b):
    return lax.dot_general(a, b, (((1,), (1,)), ((), ())), preferred_element_type=F32)


def _rms_matmul_kernel(x_ref, g_ref, w_ref, o_ref, h_ref):
    @pl.when(pl.program_id(1) == 0)
    def _():
        h_ref[...] = _rms(x_ref[...], g_ref[...]).astype(BF16)

    o_ref[...] = jnp.dot(h_ref[...], w_ref[...], preferred_element_type=F32)


def rms_matmul(x, g, w, tm, tn):
    T, K = x.shape
    N = w.shape[1]
    vmem = 2 * tm * K * 4 + tm * K * 2 + 2 * K * tn * 2 + 2 * tm * tn * 4
    return pl.pallas_call(
        _rms_matmul_kernel,
        grid=(T // tm, N // tn),
        in_specs=[pl.BlockSpec((tm, K), lambda i, j: (i, 0)),
                  pl.BlockSpec((1, K), lambda i, j: (0, 0)),
                  pl.BlockSpec((K, tn), lambda i, j: (0, j))],
        out_specs=pl.BlockSpec((tm, tn), lambda i, j: (i, j)),
        out_shape=jax.ShapeDtypeStruct((T, N), F32),
        scratch_shapes=[pltpu.VMEM((tm, K), BF16)],
        compiler_params=_cparams(("parallel", "arbitrary"), 2 * vmem),
        name="rms_matmul",
    )(x, g, w)


def _nsa_prep_kernel(kc_in, vc_in, ks_in, vs_in, kw_in, vw_in, rope_ref, gk_ref,
                     kc_out, vc_out, ks_out, vs_out, kw_out, vw_out):
    tab = rope_ref[...]
    half = NSA_ROT_DIM // 2
    kc_out[0, 0] = kc_in[...]
    vc_out[0, 0] = vc_in[...]
    ks_out[0, 0] = _rope_lanes(_rms(ks_in[...], gk_ref[1:2, :]), tab, half).astype(BF16)
    kw_out[0, 0] = _rope_lanes(_rms(kw_in[...], gk_ref[2:3, :]), tab, half).astype(BF16)
    vs_out[0, 0] = vs_in[...].astype(BF16)
    vw_out[0, 0] = vw_in[...].astype(BF16)


def nsa_prep(proj, rope_n, gk, B, S):
    ts = min(TS_PREP, S)
    ns = S // ts
    G, Dh = NSA_KV_GROUPS, NSA_HEAD_DIM
    kv_blk0 = P_KV0 // Dh

    def in_spec(branch, kv):
        off = kv_blk0 + (branch * 2 + kv) * G
        return pl.BlockSpec((ts, Dh), lambda b, s, g: (b * ns + s, off + g))

    out_spec = pl.BlockSpec((1, 1, ts, Dh), lambda b, s, g: (b, g, s, 0))
    f32_out = jax.ShapeDtypeStruct((B, G, S, Dh), F32)
    bf_out = jax.ShapeDtypeStruct((B, G, S, Dh), BF16)
    return pl.pallas_call(
        _nsa_prep_kernel,
        grid=(B, ns, G),
        in_specs=[in_spec(0, 0), in_spec(0, 1), in_spec(1, 0), in_spec(1, 1), in_spec(2, 0), in_spec(2, 1),
                  pl.BlockSpec((3, ts, Dh), lambda b, s, g: (0, b * ns + s, 0)),
                  pl.BlockSpec((N_NSA_BRANCHES, Dh), lambda b, s, g: (0, 0))],
        out_specs=[out_spec] * 6,
        out_shape=[f32_out, f32_out, bf_out, bf_out, bf_out, bf_out],
        compiler_params=_cparams(("parallel", "parallel", "parallel"), 2 * 2 * 12 * ts * Dh * 4),
        name="nsa_prep",
    )(proj, proj, proj, proj, proj, proj, rope_n, gk)


def _compress_one(t_ref, w1_ref, w2_ref, pos_ref):
    t = t_ref[0, 0]
    ncp = t.shape[0]
    u = jnp.dot((t + pos_ref[0]).astype(BF16), w1_ref[0], preferred_element_type=F32)
    v = jnp.dot((t + pos_ref[1]).astype(BF16), w1_ref[1], preferred_element_type=F32)
    pre = u + pltpu.roll(v, ncp - 1, 0)
    return jnp.dot(jax.nn.gelu(pre).astype(BF16), w2_ref[...], preferred_element_type=F32)


def _compress_kernel(tk_ref, tv_ref, w1k_ref, w2k_ref, posk_ref, w1v_ref, w2v_ref, posv_ref, rope_ref, gk_ref,
                     kc_ref, vc_ref):
    k = _compress_one(tk_ref, w1k_ref, w2k_ref, posk_ref)
    kc_ref[0, 0] = _rope_lanes(_rms(k, gk_ref[0:1, :]), rope_ref[:, 0], NSA_ROT_DIM // 2).astype(BF16)
    vc_ref[0, 0] = _compress_one(tv_ref, w1v_ref, w2v_ref, posv_ref).astype(BF16)


def nsa_compress(kc_raw, vc_raw, w1k, w2k, posk, w1v, w2v, posv, rope_c, gk):
    B, G, S, Dh = kc_raw.shape
    ncp = S // CMP_STRIDE
    kw = CMP_STRIDE * Dh
    tk = kc_raw.reshape(B, G, ncp, kw)
    tv = vc_raw.reshape(B, G, ncp, kw)
    t_spec = pl.BlockSpec((1, 1, ncp, kw), lambda b, g: (b, g, 0, 0))
    w1_spec = pl.BlockSpec((2, kw, Dh), lambda b, g: (0, 0, 0))
    w2_spec = pl.BlockSpec((Dh, Dh), lambda b, g: (0, 0))
    pos_spec = pl.BlockSpec((2, 1, kw), lambda b, g: (0, 0, 0))
    out_spec = pl.BlockSpec((1, 1, ncp, Dh), lambda b, g: (b, g, 0, 0))
    out = jax.ShapeDtypeStruct((B, G, ncp, Dh), BF16)
    return pl.pallas_call(
        _compress_kernel,
        grid=(B, G),
        in_specs=[t_spec, t_spec, w1_spec, w2_spec, pos_spec, w1_spec, w2_spec, pos_spec,
                  pl.BlockSpec((3, 1, ncp, Dh), lambda b, g: (0, b, 0, 0)),
                  pl.BlockSpec((N_NSA_BRANCHES, Dh), lambda b, g: (0, 0))],
        out_specs=[out_spec, out_spec],
        out_shape=[out, out],
        compiler_params=_cparams(("parallel", "parallel"), 2 * (4 * ncp * kw * 4 + 8 * kw * Dh * 2)),
        name="nsa_compress",
    )(tk, tv, w1k, w2k, posk, w1v, w2v, posv, rope_c, gk)


def _online_softmax(s, m, l):
    m_new = jnp.maximum(m, jnp.max(s, axis=-1, keepdims=True))
    alpha = jnp.exp2(m - m_new)
    p = jnp.exp2(s - m_new)
    return p, m_new, alpha * l + jnp.sum(p, axis=-1, keepdims=True), alpha


def _softmax_rows(s):
    p = jnp.exp2(s - jnp.max(s, axis=-1, keepdims=True))
    return p, jnp.sum(p, axis=-1, keepdims=True)


def _nsa_attn_kernel(q_ref, gl_ref, rope_ref, qg_ref, kc_ref, vc_ref, ks_ref, vs_ref, kw_ref, vw_ref, o_ref,
                     *, seq, n_sel):
    G, Hg, Dh, QB = NSA_KV_GROUPS, NSA_HEADS_PER_GROUP, NSA_HEAD_DIM, Q_BLOCK
    R = Hg * QB
    s0 = pl.program_id(1) * QB
    ncp = kc_ref.shape[2]
    nb = max(LANES, seq // SEL_BLOCK)
    tk = TK_ATT
    wk = WINDOW + QB

    tab = rope_ref[...]
    tab4 = [jnp.concatenate([tab[k]] * Hg, axis=0) for k in range(3)]
    t_q = s0 + lax.broadcasted_iota(jnp.int32, (QB, 1), 0)
    t_l = s0 + lax.broadcasted_iota(jnp.int32, (1, QB), 1)

    cmp_end = lax.broadcasted_iota(jnp.int32, (1, ncp), 1) * CMP_STRIDE + (CMP_BLOCK - 1)
    bias_c = jnp.where(cmp_end <= t_q, 0.0, NEG)
    row_ok = jnp.where(t_q >= CMP_BLOCK - 1, 1.0, 0.0)
    w0 = pl.multiple_of(jnp.maximum(s0 - WINDOW, 0), QB)
    diff = t_q - (w0 + lax.broadcasted_iota(jnp.int32, (1, wk), 1))
    bias_w = jnp.where((diff >= 0) & (diff < WINDOW), 0.0, NEG)

    j_i = lax.broadcasted_iota(jnp.int32, (nb, ncp), 0) * SEL_BLOCK
    c_i = lax.broadcasted_iota(jnp.int32, (nb, ncp), 1) * CMP_STRIDE
    overlap_t = jnp.where((c_i < j_i + SEL_BLOCK) & (c_i + (CMP_BLOCK - 1) >= j_i), 1.0, 0.0).astype(BF16)
    jt = lax.broadcasted_iota(jnp.int32, (nb, QB), 0)
    cur = jnp.right_shift(t_l, SEL_SHIFT)
    force_t = jnp.where((jt == 0) | (jt == cur) | (jt == cur - 1), SEL_FORCE, 0.0)
    eligible_t = jt <= cur
    jf = jt.astype(F32)

    qbs, sels, o_cs, o_ws = [], [], [], []
    for g in range(G):
        rows = jnp.concatenate([q_ref[:, (g * Hg + h) * Dh:(g * Hg + h + 1) * Dh] for h in range(Hg)], axis=0)
        qb = (_rope_lanes(_rms(rows, qg_ref[...]), tab4, NSA_ROT_DIM // 2) * (Dh ** -0.5 * LOG2E)).astype(BF16)
        qbs.append(qb)

        p, l = _softmax_rows(_dot_nt(qb, kc_ref[0, g]).reshape(Hg, QB, ncp) + bias_c[None])
        p = p * (row_ok[None] / l)
        o_cs.append(jnp.dot(p.reshape(R, ncp).astype(BF16), vc_ref[0, g], preferred_element_type=F32))

        imp = _dot_nt(overlap_t, jnp.sum(p, axis=0).astype(BF16))
        imp = jnp.where(eligible_t, imp + force_t, -jnp.inf)
        sel = jnp.zeros((nb, QB), F32)
        for _ in range(n_sel):
            mx = jnp.max(imp, axis=0, keepdims=True)
            first = jnp.min(jnp.where(imp == mx, jf, float(nb)), axis=0, keepdims=True)
            pick = (jf == first) & (mx > -jnp.inf)
            sel = jnp.where(pick, 1.0, sel)
            imp = jnp.where(pick, -jnp.inf, imp)
        sels.append(sel.T.astype(BF16))

        p, l = _softmax_rows(_dot_nt(qb, kw_ref[0, g, pl.ds(w0, wk), :]).reshape(Hg, QB, wk) + bias_w[None])
        o_w = jnp.dot(p.reshape(R, wk).astype(BF16), vw_ref[0, g, pl.ds(w0, wk), :], preferred_element_type=F32)
        o_ws.append(o_w / l.reshape(R, 1))

    jb = lax.broadcasted_iota(jnp.int32, (nb, tk), 0)
    kl = lax.broadcasted_iota(jnp.int32, (nb, tk), 1)
    kl_row = lax.broadcasted_iota(jnp.int32, (1, tk), 1)

    def sel_step(kt, carry):
        k0 = pl.multiple_of(kt * tk, tk)
        expand = jnp.where(jb == jnp.right_shift(k0 + kl, SEL_SHIFT), 1.0, 0.0).astype(BF16)
        causal = k0 + kl_row <= t_q
        out = []
        for g in range(G):
            m, l, acc = carry[g]
            chosen = jnp.dot(sels[g], expand, preferred_element_type=F32)
            bias = jnp.where((chosen > 0.5) & causal, 0.0, NEG)
            s = _dot_nt(qbs[g], ks_ref[0, g, pl.ds(k0, tk), :]).reshape(Hg, QB, tk) + bias[None]
            p, m, l, alpha = _online_softmax(s, m, l)
            pv = jnp.dot(p.reshape(R, tk).astype(BF16), vs_ref[0, g, pl.ds(k0, tk), :], preferred_element_type=F32)
            out.append((m, l, alpha * acc + pv.reshape(Hg, QB, Dh)))
        return tuple(out)

    init = tuple((jnp.full((Hg, QB, 1), NEG, F32), jnp.zeros((Hg, QB, 1), F32), jnp.zeros((Hg, QB, Dh), F32))
                 for _ in range(G))
    final = lax.fori_loop(0, (s0 + QB + tk - 1) // tk, sel_step, init)

    gates = jax.nn.sigmoid(gl_ref[...])
    for g in range(G):
        o_s = (final[g][2] / final[g][1]).reshape(R, Dh)
        for h in range(Hg):
            c0 = (g * Hg + h) * N_NSA_BRANCHES
            rs = slice(h * QB, (h + 1) * QB)
            o_ref[:, (g * Hg + h) * Dh:(g * Hg + h + 1) * Dh] = (
                gates[:, c0:c0 + 1] * o_cs[g][rs] + gates[:, c0 + 1:c0 + 2] * o_s[rs]
                + gates[:, c0 + 2:c0 + 3] * o_ws[g][rs])


def nsa_attention(proj, rope_n, qg, kc, vc, ks, vs, kw, vw, B, S):
    G, Hg, Dh, QB = NSA_KV_GROUPS, NSA_HEADS_PER_GROUP, NSA_HEAD_DIM, Q_BLOCK
    nq = S // QB
    ncp = kc.shape[2]
    n_sel = min(N_SEL_BLOCKS, S // SEL_BLOCK)
    qcols = G * Hg * Dh
    kv_spec = pl.BlockSpec((1, G, S, Dh), lambda b, i: (b, 0, 0, 0))
    c_spec = pl.BlockSpec((1, G, ncp, Dh), lambda b, i: (b, 0, 0, 0))
    sw = max(TK_ATT, ncp, WINDOW + QB)
    vmem = 2 * G * (4 * S * Dh * 2 + 2 * ncp * Dh * 2) + 8 * G * Hg * QB * sw * 4
    return pl.pallas_call(
        functools.partial(_nsa_attn_kernel, seq=S, n_sel=n_sel),
        grid=(B, nq),
        in_specs=[pl.BlockSpec((QB, qcols), lambda b, i: (b * nq + i, P_Q0 // qcols)),
                  pl.BlockSpec((QB, LANES), lambda b, i: (b * nq + i, P_G0 // LANES)),
                  pl.BlockSpec((3, QB, Dh), lambda b, i: (0, b * nq + i, 0)),
                  pl.BlockSpec((1, Dh), lambda b, i: (0, 0)),
                  c_spec, c_spec, kv_spec, kv_spec, kv_spec, kv_spec],
        out_specs=pl.BlockSpec((QB, qcols), lambda b, i: (b * nq + i, 0)),
        out_shape=jax.ShapeDtypeStruct((B * S, qcols), F32),
        compiler_params=_cparams(("parallel", "arbitrary"), vmem),
        name="nsa_attention",
    )(proj, proj, rope_n, qg, kc, vc, ks, vs, kw, vw)


def _mla_q_kernel(qa_ref, ga_ref, w_ref, gq_ref, rope_ref, o_ref, c_ref):
    @pl.when(pl.program_id(1) == 0)
    def _():
        c_ref[...] = _rms(qa_ref[...], ga_ref[...]).astype(BF16)

    q = jnp.dot(c_ref[...], w_ref[...], preferred_element_type=F32)
    ms = jnp.sum(q * q, axis=-1, keepdims=True) * (1.0 / MLA_QK_DIM)
    qn = q * lax.rsqrt(ms + NORM_EPS) * gq_ref[...]
    fold = MLA_QK_DIM ** -0.5 * LOG2E
    o_ref[0, 0, :, :LANES] = (qn[:, :LANES] * fold).astype(BF16)
    o_ref[0, 0, :, LANES:] = (_rope_lanes(qn[:, LANES:], rope_ref[...], MLA_ROPE_DIM // 2) * fold).astype(BF16)


def mla_q_prep(proj, ga, wq, gq, rope_m, B, S):
    tm = min(TM, S)
    ns = S // tm
    H, R, W = MLA_HEADS, MLA_Q_RANK, MLA_QK_PAD
    return pl.pallas_call(
        _mla_q_kernel,
        grid=(B * ns, H),
        in_specs=[pl.BlockSpec((tm, R), lambda r, h: (r, P_QA0 // R)),
                  pl.BlockSpec((1, R), lambda r, h: (0, 0)),
                  pl.BlockSpec((R, W), lambda r, h: (0, h)),
                  pl.BlockSpec((1, W), lambda r, h: (0, 0)),
                  pl.BlockSpec((3, tm, LANES), lambda r, h: (0, r, 0))],
        out_specs=pl.BlockSpec((1, 1, tm, W), lambda r, h: (r // ns, h, r % ns, 0)),
        out_shape=jax.ShapeDtypeStruct((B, H, S, W), BF16),
        scratch_shapes=[pltpu.VMEM((tm, R), BF16)],
        compiler_params=_cparams(("parallel", "arbitrary"), 4 * (tm * R * 4 + R * W * 2 + tm * W * 6)),
        name="mla_q_prep",
    )(proj, ga, wq, gq, rope_m)


def _mla_kv_kernel(kva_ref, kr_ref, ga_ref, w_ref, gk_ref, rope_ref, k_ref, v_ref, c_ref):
    @pl.when(pl.program_id(1) == 0)
    def _():
        c_ref[...] = _rms(kva_ref[...], ga_ref[...]).astype(BF16)

    kv = jnp.dot(c_ref[...], w_ref[...], preferred_element_type=F32)
    k_nope = kv[:, :LANES]
    k_rot = kr_ref[...]
    ms = (jnp.sum(k_nope * k_nope, axis=-1, keepdims=True)
          + jnp.sum(k_rot * k_rot, axis=-1, keepdims=True)) * (1.0 / MLA_QK_DIM)
    r = lax.rsqrt(ms + NORM_EPS)
    k_ref[0, 0, :, :LANES] = (k_nope * r * gk_ref[:, :LANES]).astype(BF16)
    k_ref[0, 0, :, LANES:] = _rope_lanes(k_rot * r * gk_ref[:, LANES:], rope_ref[...], MLA_ROPE_DIM // 2).astype(BF16)
    v_ref[0, 0] = kv[:, LANES:].astype(BF16)


def mla_kv_prep(proj, ga, wkv, gk, rope_m, B, S):
    tm = min(TM, S)
    ns = S // tm
    H, R, W = MLA_HEADS, MLA_KV_RANK, MLA_QK_PAD
    return pl.pallas_call(
        _mla_kv_kernel,
        grid=(B * ns, H),
        in_specs=[pl.BlockSpec((tm, R), lambda r, h: (r, P_KVA0 // R)),
                  pl.BlockSpec((tm, LANES), lambda r, h: (r, P_KR0 // LANES)),
                  pl.BlockSpec((1, R), lambda r, h: (0, 0)),
                  pl.BlockSpec((R, W), lambda r, h: (0, h)),
                  pl.BlockSpec((1, W), lambda r, h: (0, 0)),
                  pl.BlockSpec((3, tm, LANES), lambda r, h: (0, r, 0))],
        out_specs=[pl.BlockSpec((1, 1, tm, W), lambda r, h: (r // ns, h, r % ns, 0)),
                   pl.BlockSpec((1, 1, tm, MLA_V_DIM), lambda r, h: (r // ns, h, r % ns, 0))],
        out_shape=[jax.ShapeDtypeStruct((B, H, S, W), BF16), jax.ShapeDtypeStruct((B, H, S, MLA_V_DIM), BF16)],
        scratch_shapes=[pltpu.VMEM((tm, R), BF16)],
        compiler_params=_cparams(("parallel", "arbitrary"), 4 * (tm * R * 4 + R * W * 2 + tm * W * 8)),
        name="mla_kv_prep",
    )(proj, proj, ga, wkv, gk, rope_m)


def _mla_attn_kernel(q_ref, k_ref, v_ref, o_ref, *, tq, heads):
    i = pl.program_id(2)
    Dv = v_ref.shape[-1]

    def tile(k0, carry, bias):
        out = []
        for c in range(heads):
            m, l, acc = carry[c]
            s = _dot_nt(q_ref[0, c], k_ref[0, c, pl.ds(k0, tq), :])
            if bias is not None:
                s = s + bias
            p, m, l, alpha = _online_softmax(s, m, l)
            pv = jnp.dot(p.astype(BF16), v_ref[0, c, pl.ds(k0, tq), :], preferred_element_type=F32)
            out.append((m, l, alpha * acc + pv))
        return tuple(out)

    init = tuple((jnp.full((tq, 1), NEG, F32), jnp.zeros((tq, 1), F32), jnp.zeros((tq, Dv), F32))
                 for _ in range(heads))
    carry = lax.fori_loop(0, i, lambda kt, c: tile(pl.multiple_of(kt * tq, tq), c, None), init)
    causal = jnp.where(lax.broadcasted_iota(jnp.int32, (tq, tq), 1) <= lax.broadcasted_iota(jnp.int32, (tq, tq), 0),
                       0.0, NEG)
    carry = tile(pl.multiple_of(i * tq, tq), carry, causal)
    for c in range(heads):
        o_ref[:, c * Dv:(c + 1) * Dv] = carry[c][2] / carry[c][1]


def mla_attention(q, k, v):
    B, H, S, W = q.shape
    tq = min(TQ_MLA, S)
    nq = S // tq
    Dv = v.shape[-1]
    hp = MLA_HEADS_PER_STEP
    vmem = hp * (2 * (S * W * 2 + S * Dv * 2) + 4 * tq * W * 2 + 8 * tq * tq * 4)
    return pl.pallas_call(
        functools.partial(_mla_attn_kernel, tq=tq, heads=hp),
        grid=(B, H // hp, nq),
        in_specs=[pl.BlockSpec((1, hp, tq, W), lambda b, h, i: (b, h, i, 0)),
                  pl.BlockSpec((1, hp, S, W), lambda b, h, i: (b, h, 0, 0)),
                  pl.BlockSpec((1, hp, S, Dv), lambda b, h, i: (b, h, 0, 0))],
        out_specs=pl.BlockSpec((tq, hp * Dv), lambda b, h, i: (b * nq + i, h)),
        out_shape=jax.ShapeDtypeStruct((B * S, H * Dv), F32),
        compiler_params=_cparams(("parallel", "parallel", "arbitrary"), vmem),
        name="mla_attention",
    )(q, k, v)


def _out_proj_kernel(x_ref, a_ref, b_ref, ga_ref, gb_ref, wa_ref, wb_ref, o_ref, na_ref, nb_ref):
    @pl.when(pl.program_id(1) == 0)
    def _():
        na_ref[...] = _rms(a_ref[...], ga_ref[...]).astype(BF16)
        nb_ref[...] = _rms(b_ref[...], gb_ref[...]).astype(BF16)

    o_ref[...] = (x_ref[...] + jnp.dot(na_ref[...], wa_ref[...], preferred_element_type=F32)
                  + jnp.dot(nb_ref[...], wb_ref[...], preferred_element_type=F32))


def out_proj(x, o_nsa, o_mla, g_nsa, g_mla, w_out):
    T, D = x.shape
    Ka, Kb = o_nsa.shape[1], o_mla.shape[1]
    assert Ka == Kb
    tm, tn = min(TM, T), TN_OUT
    vmem = 2 * (tm * tn * 8 + 2 * tm * Ka * 4 + 2 * Ka * tn * 2) + 2 * tm * Ka * 2
    return pl.pallas_call(
        _out_proj_kernel,
        grid=(T // tm, D // tn),
        in_specs=[pl.BlockSpec((tm, tn), lambda i, j: (i, j)),
                  pl.BlockSpec((tm, Ka), lambda i, j: (i, 0)),
                  pl.BlockSpec((tm, Kb), lambda i, j: (i, 0)),
                  pl.BlockSpec((1, Ka), lambda i, j: (0, 0)),
                  pl.BlockSpec((1, Kb), lambda i, j: (0, 0)),
                  pl.BlockSpec((Ka, tn), lambda i, j: (0, j)),
                  pl.BlockSpec((Kb, tn), lambda i, j: (1, j))],
        out_specs=pl.BlockSpec((tm, tn), lambda i, j: (i, j)),
        out_shape=jax.ShapeDtypeStruct((T, D), F32),
        scratch_shapes=[pltpu.VMEM((tm, Ka), BF16), pltpu.VMEM((tm, Kb), BF16)],
        compiler_params=_cparams(("parallel", "arbitrary"), 2 * vmem),
        name="out_proj",
    )(x, o_nsa, o_mla, g_nsa, g_mla, w_out, w_out)


def _swiglu_kernel(be_ref, x_ref, g_ref, wg_ref, wu_ref, wd_ref, o_ref, h_ref, acc_ref, *, residual):
    i = pl.program_id(0)
    f = pl.program_id(1)
    live = i < be_ref[0]

    @pl.when(live & (f == 0))
    def _():
        h_ref[...] = _rms(x_ref[...], g_ref[...]).astype(BF16)
        acc_ref[...] = jnp.zeros_like(acc_ref)

    @pl.when(live)
    def _():
        h = h_ref[...]
        a = jax.nn.silu(jnp.dot(h, wg_ref[0], preferred_element_type=F32)) * jnp.dot(
            h, wu_ref[0], preferred_element_type=F32)
        acc_ref[...] += jnp.dot(a.astype(BF16), wd_ref[0], preferred_element_type=F32)

    last = f == pl.num_programs(1) - 1

    @pl.when(live & last)
    def _():
        o_ref[...] = x_ref[...] + acc_ref[...] if residual else acc_ref[...]

    @pl.when(jnp.logical_not(live) & last)
    def _():
        o_ref[...] = jnp.zeros_like(o_ref)


def swiglu_blocks(x, g, wg, wu, wd, block_info, tm, residual):
    R, D = x.shape
    F = wg.shape[2]
    tf = TF
    nf = F // tf

    def wcol(i, f, be):
        live = i < be[0]
        return (be[1 + i], 0, jnp.where(live, f, nf - 1))

    def wrow(i, f, be):
        live = i < be[0]
        return (be[1 + i], jnp.where(live, f, nf - 1), 0)

    vmem = 4 * tm * D * 4 + tm * D * 2 + tm * D * 4 + 2 * 3 * D * tf * 2 + 6 * tm * tf * 4
    grid_spec = pltpu.PrefetchScalarGridSpec(
        num_scalar_prefetch=1,
        grid=(R // tm, nf),
        in_specs=[pl.BlockSpec((tm, D), lambda i, f, be: (i, 0)),
                  pl.BlockSpec((1, D), lambda i, f, be: (0, 0)),
                  pl.BlockSpec((1, D, tf), wcol),
                  pl.BlockSpec((1, D, tf), wcol),
                  pl.BlockSpec((1, tf, D), wrow)],
        out_specs=pl.BlockSpec((tm, D), lambda i, f, be: (i, 0)),
        scratch_shapes=[pltpu.VMEM((tm, D), BF16), pltpu.VMEM((tm, D), F32)],
    )
    return pl.pallas_call(
        functools.partial(_swiglu_kernel, residual=residual),
        grid_spec=grid_spec,
        out_shape=jax.ShapeDtypeStruct((R, D), F32),
        compiler_params=_cparams(("parallel", "arbitrary"), vmem + (4 << 20)),
        name="swiglu",
    )(block_info, x, g, wg, wu, wd)


def _router_kernel(x_ref, g_ref, w_ref, o_ref):
    h = _rms(x_ref[...], g_ref[...]).astype(BF16)
    logits = jnp.dot(h, w_ref[...], preferred_element_type=F32)
    lane = lax.broadcasted_iota(jnp.int32, logits.shape, 1)
    lf = lane.astype(F32)
    lg = jnp.where(lane < N_EXPERTS, logits, -jnp.inf)
    m1 = jnp.max(lg, axis=-1, keepdims=True)
    i1 = jnp.min(jnp.where(lg == m1, lf, float(LANES)), axis=-1, keepdims=True)
    lg2 = jnp.where(lf == i1, -jnp.inf, lg)
    m2 = jnp.max(lg2, axis=-1, keepdims=True)
    i2 = jnp.min(jnp.where(lg2 == m2, lf, float(LANES)), axis=-1, keepdims=True)
    e2 = jnp.exp(m2 - m1)
    den = 1.0 + e2
    o_ref[...] = jnp.where(lane == 0, i1, jnp.where(lane == 1, i2, jnp.where(lane == 2, 1.0 / den, e2 / den)))


def moe_router(x, g, w_router_p):
    T, D = x.shape
    tm = min(TM, T)
    return pl.pallas_call(
        _router_kernel,
        grid=(T // tm,),
        in_specs=[pl.BlockSpec((tm, D), lambda i: (i, 0)),
                  pl.BlockSpec((1, D), lambda i: (0, 0)),
                  pl.BlockSpec((D, LANES), lambda i: (0, 0))],
        out_specs=pl.BlockSpec((tm, LANES), lambda i: (i, 0)),
        out_shape=jax.ShapeDtypeStruct((T, LANES), F32),
        compiler_params=_cparams(("parallel",), 4 * tm * D * 4),
        name="moe_router",
    )(x, g, w_router_p)


def _gather_kernel(idx_ref, src_ref, dst_ref, sem):
    base = pl.program_id(0) * GATHER_ROWS

    def row_copy(r):
        return pltpu.make_async_copy(src_ref.at[idx_ref[base + r]], dst_ref.at[base + r], sem)

    def start(r, c):
        row_copy(r).start()
        return c

    def wait(r, c):
        row_copy(r).wait()
        return c

    lax.fori_loop(0, GATHER_ROWS, start, 0)
    lax.fori_loop(0, GATHER_ROWS, wait, 0)


def gather_rows(src, idx):
    n = idx.shape[0]
    D = src.shape[1]
    src = src.reshape(src.shape[0], D // LANES, LANES)
    grid_spec = pltpu.PrefetchScalarGridSpec(
        num_scalar_prefetch=1,
        grid=(n // GATHER_ROWS,),
        in_specs=[pl.BlockSpec(memory_space=pl.ANY)],
        out_specs=pl.BlockSpec(memory_space=pl.ANY),
        scratch_shapes=[pltpu.SemaphoreType.DMA(())],
    )
    return pl.pallas_call(
        _gather_kernel,
        grid_spec=grid_spec,
        out_shape=jax.ShapeDtypeStruct((n, D // LANES, LANES), src.dtype),
        compiler_params=pltpu.CompilerParams(dimension_semantics=("arbitrary",)),
        name="gather_rows",
    )(idx, src).reshape(n, D)


def _combine_kernel(x_ref, ya_ref, yb_ref, gate_ref, o_ref):
    gate = gate_ref[...]
    o_ref[...] = x_ref[...] + (ya_ref[...] * gate[:, 2:3] + yb_ref[...] * gate[:, 3:4])


def moe_combine(x, y_pair, route):
    T, D = x.shape
    tm = min(TM, T)
    nt = T // tm
    return pl.pallas_call(
        _combine_kernel,
        grid=(nt,),
        in_specs=[pl.BlockSpec((tm, D), lambda i: (i, 0)),
                  pl.BlockSpec((tm, D), lambda i: (i, 0)),
                  pl.BlockSpec((tm, D), lambda i: (i + nt, 0)),
                  pl.BlockSpec((tm, LANES), lambda i: (i, 0))],
        out_specs=pl.BlockSpec((tm, D), lambda i: (i, 0)),
        out_shape=jax.ShapeDtypeStruct((T, D), F32),
        compiler_params=_cparams(("parallel",), 10 * tm * D * 4),
        name="moe_combine",
    )(x, y_pair, y_pair, route)


def moe_ffn(x, g, w_router_p, wg, wu, wd):
    T, D = x.shape
    n_assign = T * TOP_K
    route = moe_router(x, g, w_router_p)
    flat_e = route[:, :TOP_K].astype(jnp.int32).reshape(-1)
    order = jnp.argsort(flat_e)
    se = flat_e[order]
    stok = (order // TOP_K).astype(jnp.int32)
    counts = jnp.bincount(flat_e, length=N_EXPERTS)
    padded = (counts + MOE_TM - 1) // MOE_TM * MOE_TM
    pad_end = jnp.cumsum(padded)
    pad_start = pad_end - padded
    start = jnp.cumsum(counts) - counts
    dest = (pad_start[se] + jnp.arange(n_assign) - start[se]).astype(jnp.int32)
    n_blocks = -(-(n_assign + N_EXPERTS * (MOE_TM - 1)) // MOE_TM)
    n_rows = n_blocks * MOE_TM
    row_tok = jnp.zeros((n_rows,), jnp.int32).at[dest].set(stok)
    block_e = jnp.minimum(jnp.searchsorted(pad_end, jnp.arange(n_blocks) * MOE_TM, side='right'), N_EXPERTS - 1)
    n_live = (pad_end[-1] // MOE_TM).astype(jnp.int32)
    block_info = jnp.concatenate([n_live[None], block_e.astype(jnp.int32)])
    dest_flat = jnp.zeros((n_assign,), jnp.int32).at[order].set(dest)
    pair_idx = jnp.concatenate([dest_flat[0::2], dest_flat[1::2]])

    x_rows = gather_rows(x, row_tok)
    y_rows = swiglu_blocks(x_rows, g, wg, wu, wd, block_info, MOE_TM, residual=False)
    y_pair = gather_rows(y_rows, pair_idx)
    return moe_combine(x, y_pair, route)


def _rope_tab(pos, rot_dim):
    half = rot_dim // 2
    inv_freq = jnp.power(jnp.float32(ROPE_THETA), -jnp.arange(0, rot_dim, 2, dtype=F32) / rot_dim)
    ang = pos.astype(F32)[..., None] * inv_freq
    cos, sin = jnp.cos(ang), jnp.sin(ang)
    rest = LANES - rot_dim
    one = jnp.ones(pos.shape + (rest,), F32)
    zr = jnp.zeros(pos.shape + (rest,), F32)
    zh = jnp.zeros_like(sin)
    return jnp.stack([jnp.concatenate([cos, cos, one], -1),
                      jnp.concatenate([zh, sin, zr], -1),
                      jnp.concatenate([-sin, zh, zr], -1)])


def _pad_cols(w, n):
    return jnp.pad(w, ((0, 0), (0, n - w.shape[1])))


def kernel(x, positions, attn_norm, w_in, nsa_q_norm, nsa_k_norm, cmp_k_w1, cmp_k_w2, cmp_k_pos, cmp_v_w1, cmp_v_w2, cmp_v_pos, mla_q_a_norm, mla_w_q_up, mla_kv_a_norm, mla_w_kv_up, mla_q_norm, mla_k_norm, nsa_out_norm, mla_out_norm, w_out, ffn_norm, dense_w_gate, dense_w_up, dense_w_down, moe_router, moe_w_gate, moe_w_up, moe_w_down):
    B, S, D = x.shape
    T = B * S
    depth = w_in.shape[0]
    Dh = NSA_HEAD_DIM
    ncp = S // CMP_STRIDE
    kw = CMP_STRIDE * Dh

    rope_n = _rope_tab(positions, NSA_ROT_DIM).reshape(3, T, LANES)
    rope_m = _rope_tab(positions, MLA_ROPE_DIM).reshape(3, T, LANES)
    cmp_idx = jnp.minimum(jnp.arange(ncp) * CMP_STRIDE + CMP_BLOCK - 1, S - 1)
    rope_c = _rope_tab(positions[:, cmp_idx], NSA_ROT_DIM)

    g0 = NSA_Q_COLS + NSA_KV_COLS
    m0 = g0 + NSA_GATE_COLS
    xt = x.reshape(T, D)
    for layer in range(depth):
        w = w_in[layer]
        w_in_p = jnp.concatenate([w[:, :g0], w[:, m0:m0 + MLA_Q_RANK], w[:, m0 + MLA_Q_RANK:m0 + MLA_Q_RANK + MLA_KV_RANK],
                                  _pad_cols(w[:, m0 + MLA_Q_RANK + MLA_KV_RANK:], LANES),
                                  _pad_cols(w[:, g0:m0], LANES)], axis=1).astype(BF16)
        proj = rms_matmul(xt, attn_norm[layer][None], w_in_p, min(TM, T), TN_PROJ)

        gk = nsa_k_norm[layer]
        kc_raw, vc_raw, ks, vs, kwin, vwin = nsa_prep(proj, rope_n, gk, B, S)
        kc, vc = nsa_compress(
            kc_raw, vc_raw,
            cmp_k_w1[layer].reshape(2, kw, Dh).astype(BF16), cmp_k_w2[layer].astype(BF16),
            cmp_k_pos[layer].reshape(2, 1, kw),
            cmp_v_w1[layer].reshape(2, kw, Dh).astype(BF16), cmp_v_w2[layer].astype(BF16),
            cmp_v_pos[layer].reshape(2, 1, kw), rope_c, gk)
        o_nsa = nsa_attention(proj, rope_n, nsa_q_norm[layer][None], kc, vc, ks, vs, kwin, vwin, B, S)

        wq = jnp.pad(mla_w_q_up[layer].reshape(MLA_Q_RANK, MLA_HEADS, MLA_QK_DIM),
                     ((0, 0), (0, 0), (0, MLA_QK_PAD - MLA_QK_DIM))).reshape(MLA_Q_RANK, MLA_HEADS * MLA_QK_PAD)
        gq = jnp.pad(mla_q_norm[layer], (0, MLA_QK_PAD - MLA_QK_DIM))[None]
        gkm = jnp.pad(mla_k_norm[layer], (0, MLA_QK_PAD - MLA_QK_DIM))[None]
        q_m = mla_q_prep(proj, mla_q_a_norm[layer][None], wq.astype(BF16), gq, rope_m, B, S)
        k_m, v_m = mla_kv_prep(proj, mla_kv_a_norm[layer][None], mla_w_kv_up[layer].astype(BF16), gkm, rope_m, B, S)
        o_mla = mla_attention(q_m, k_m, v_m)

        xt = out_proj(xt, o_nsa, o_mla, nsa_out_norm[layer][None], mla_out_norm[layer][None],
                      w_out[layer].astype(BF16))

        i = layer // 2
        gf = ffn_norm[layer][None]
        if layer % 2 == 0:
            tm = min(TM, T)
            info = jnp.concatenate([jnp.full((1,), T // tm, jnp.int32), jnp.zeros((T // tm,), jnp.int32)])
            xt = swiglu_blocks(xt, gf, dense_w_gate[i][None].astype(BF16), dense_w_up[i][None].astype(BF16),
                               dense_w_down[i][None].astype(BF16), info, tm, residual=True)
        else:
            xt = moe_ffn(xt, gf, _pad_cols(moe_router[i], LANES).astype(BF16), moe_w_gate[i].astype(BF16),
                         moe_w_up[i].astype(BF16), moe_w_down[i].astype(BF16))
    return xt.reshape(B, S, D)
```

```python
import functools

import jax
import jax.numpy as jnp
from jax import lax
from jax.experimental import pallas as pl
from jax.experimental.pallas import tpu as pltpu

F32 = jnp.float32
BF16 = jnp.bfloat16

LANES = 128
VMEM_CAP_BYTES = 56 * 1024 * 1024

LOG2E = 1.4426950408889634
NEG = -1e30

NORM_EPS = 1e-6
ROPE_THETA = 500000.0
Q_BLOCK = 128

NSA_HEAD_DIM = 128
NSA_KV_GROUPS = 2
NSA_HEADS_PER_GROUP = 4
NSA_HEADS = NSA_KV_GROUPS * NSA_HEADS_PER_GROUP
NSA_ROT_DIM = NSA_HEAD_DIM // 4
N_NSA_BRANCHES = 3
CMP_BLOCK = 32
CMP_STRIDE = 16
SEL_BLOCK = 64
SEL_SHIFT = SEL_BLOCK.bit_length() - 1
assert 1 << SEL_SHIFT == SEL_BLOCK
N_SEL_BLOCKS = 16
WINDOW = 512
SEL_FORCE = 1.0e4

MLA_V_DIM = 128
MLA_HEADS = 8
MLA_Q_RANK = 512
MLA_KV_RANK = 512
MLA_NOPE_DIM = 128
MLA_ROPE_DIM = 64
MLA_QK_DIM = MLA_NOPE_DIM + MLA_ROPE_DIM
MLA_QK_PAD = 256

N_EXPERTS = 8
TOP_K = 2

NSA_Q_COLS = NSA_HEADS * NSA_HEAD_DIM
NSA_KV_COLS = N_NSA_BRANCHES * 2 * NSA_KV_GROUPS * NSA_HEAD_DIM
NSA_GATE_COLS = NSA_HEADS * N_NSA_BRANCHES
P_Q0 = 0
P_KV0 = P_Q0 + NSA_Q_COLS
P_QA0 = P_KV0 + NSA_KV_COLS
P_KVA0 = P_QA0 + MLA_Q_RANK
P_KR0 = P_KVA0 + MLA_KV_RANK
P_G0 = P_KR0 + LANES
P_COLS = P_G0 + LANES

TM = 512
TN_PROJ = 768
TN_OUT = 512
TF = 512
TS_PREP = 1024
TK_ATT = 512
TQ_MLA = 512
MLA_HEADS_PER_STEP = 2
MOE_TM = 512
GATHER_ROWS = 512


def _cparams(sem, vmem_bytes):
    return pltpu.CompilerParams(dimension_semantics=sem,
                                vmem_limit_bytes=int(min(VMEM_CAP_BYTES, max(vmem_bytes, 16 * 1024 * 1024))))


def _rms(x, g):
    ms = jnp.mean(x * x, axis=-1, keepdims=True)
    return x * lax.rsqrt(ms + NORM_EPS) * g


def _rope_lanes(x, tab, half):
    return x * tab[0] + pltpu.roll(x, half, 1) * tab[1] + pltpu.roll(x, LANES - half, 1) * tab[2]


def _dot_nt(a, b):
    return lax.dot_general(a, b, (((1,), (1,)), ((), ())), preferred_element_type=F32)


def _rms_matmul_kernel(x_ref, g_ref, w_ref, o_ref, h_ref):
    @pl.when(pl.program_id(1) == 0)
    def _():
        h_ref[...] = _rms(x_ref[...], g_ref[...]).astype(BF16)

    o_ref[...] = jnp.dot(h_ref[...], w_ref[...], preferred_element_type=F32)


def rms_matmul(x, g, w, tm, tn):
    T, K = x.shape
    N = w.shape[1]
    vmem = 2 * tm * K * 4 + tm * K * 2 + 2 * K * tn * 2 + 2 * tm * tn * 4
    return pl.pallas_call(
        _rms_matmul_kernel,
        grid=(T // tm, N // tn),
        in_specs=[pl.BlockSpec((tm, K), lambda i, j: (i, 0)),
                  pl.BlockSpec((1, K), lambda i, j: (0, 0)),
                  pl.BlockSpec((K, tn), lambda i, j: (0, j))],
        out_specs=pl.BlockSpec((tm, tn), lambda i, j: (i, j)),
        out_shape=jax.ShapeDtypeStruct((T, N), F32),
        scratch_shapes=[pltpu.VMEM((tm, K), BF16)],
        compiler_params=_cparams(("parallel", "arbitrary"), 2 * vmem),
        name="rms_matmul",
    )(x, g, w)


def _nsa_prep_kernel(kc_in, vc_in, ks_in, vs_in, kw_in, vw_in, rope_ref, gk_ref,
                     kc_out, vc_out, ks_out, vs_out, kw_out, vw_out):
    tab = rope_ref[...]
    half = NSA_ROT_DIM // 2
    kc_out[0, 0] = kc_in[...]
    vc_out[0, 0] = vc_in[...]
    ks_out[0, 0] = _rope_lanes(_rms(ks_in[...], gk_ref[1:2, :]), tab, half).astype(BF16)
    kw_out[0, 0] = _rope_lanes(_rms(kw_in[...], gk_ref[2:3, :]), tab, half).astype(BF16)
    vs_out[0, 0] = vs_in[...].astype(BF16)
    vw_out[0, 0] = vw_in[...].astype(BF16)


def nsa_prep(proj, rope_n, gk, B, S):
    ts = min(TS_PREP, S)
    ns = S // ts
    G, Dh = NSA_KV_GROUPS, NSA_HEAD_DIM
    kv_blk0 = P_KV0 // Dh

    def in_spec(branch, kv):
        off = kv_blk0 + (branch * 2 + kv) * G
        return pl.BlockSpec((ts, Dh), lambda b, s, g: (b * ns + s, off + g))

    out_spec = pl.BlockSpec((1, 1, ts, Dh), lambda b, s, g: (b, g, s, 0))
    f32_out = jax.ShapeDtypeStruct((B, G, S, Dh), F32)
    bf_out = jax.ShapeDtypeStruct((B, G, S, Dh), BF16)
    return pl.pallas_call(
        _nsa_prep_kernel,
        grid=(B, ns, G),
        in_specs=[in_spec(0, 0), in_spec(0, 1), in_spec(1, 0), in_spec(1, 1), in_spec(2, 0), in_spec(2, 1),
                  pl.BlockSpec((3, ts, Dh), lambda b, s, g: (0, b * ns + s, 0)),
                  pl.BlockSpec((N_NSA_BRANCHES, Dh), lambda b, s, g: (0, 0))],
        out_specs=[out_spec] * 6,
        out_shape=[f32_out, f32_out, bf_out, bf_out, bf_out, bf_out],
        compiler_params=_cparams(("parallel", "parallel", "parallel"), 2 * 2 * 12 * ts * Dh * 4),
        name="nsa_prep",
    )(proj, proj, proj, proj, proj, proj, rope_n, gk)


def _compress_one(t_ref, w1_ref, w2_ref, pos_ref):
    t = t_ref[0, 0]
    ncp = t.shape[0]
    u = jnp.dot((t + pos_ref[0]).astype(BF16), w1_ref[0], preferred_element_type=F32)
    v = jnp.dot((t + pos_ref[1]).astype(BF16), w1_ref[1], preferred_element_type=F32)
    pre = u + pltpu.roll(v, ncp - 1, 0)
    return jnp.dot(jax.nn.gelu(pre).astype(BF16), w2_ref[...], preferred_element_type=F32)


def _compress_kernel(tk_ref, tv_ref, w1k_ref, w2k_ref, posk_ref, w1v_ref, w2v_ref, posv_ref, rope_ref, gk_ref,
                     kc_ref, vc_ref):
    k = _compress_one(tk_ref, w1k_ref, w2k_ref, posk_ref)
    kc_ref[0, 0] = _rope_lanes(_rms(k, gk_ref[0:1, :]), rope_ref[:, 0], NSA_ROT_DIM // 2).astype(BF16)
    vc_ref[0, 0] = _compress_one(tv_ref, w1v_ref, w2v_ref, posv_ref).astype(BF16)


def nsa_compress(kc_raw, vc_raw, w1k, w2k, posk, w1v, w2v, posv, rope_c, gk):
    B, G, S, Dh = kc_raw.shape
    ncp = S // CMP_STRIDE
    kw = CMP_STRIDE * Dh
    tk = kc_raw.reshape(B, G, ncp, kw)
    tv = vc_raw.reshape(B, G, ncp, kw)
    t_spec = pl.BlockSpec((1, 1, ncp, kw), lambda b, g: (b, g, 0, 0))
    w1_spec = pl.BlockSpec((2, kw, Dh), lambda b, g: (0, 0, 0))
    w2_spec = pl.BlockSpec((Dh, Dh), lambda b, g: (0, 0))
    pos_spec = pl.BlockSpec((2, 1, kw), lambda b, g: (0, 0, 0))
    out_spec = pl.BlockSpec((1, 1, ncp, Dh), lambda b, g: (b, g, 0, 0))
    out = jax.ShapeDtypeStruct((B, G, ncp, Dh), BF16)
    return pl.pallas_call(
        _compress_kernel,
        grid=(B, G),
        in_specs=[t_spec, t_spec, w1_spec, w2_spec, pos_spec, w1_spec, w2_spec, pos_spec,
                  pl.BlockSpec((3, 1, ncp, Dh), lambda b, g: (0, b, 0, 0)),
                  pl.BlockSpec((N_NSA_BRANCHES, Dh), lambda b, g: (0, 0))],
        out_specs=[out_spec, out_spec],
        out_shape=[out, out],
        compiler_params=_cparams(("parallel", "parallel"), 2 * (4 * ncp * kw * 4 + 8 * kw * Dh * 2)),
        name="nsa_compress",
    )(tk, tv, w1k, w2k, posk, w1v, w2v, posv, rope_c, gk)


def _online_softmax(s, m, l):
    m_new = jnp.maximum(m, jnp.max(s, axis=-1, keepdims=True))
    alpha = jnp.exp2(m - m_new)
    p = jnp.exp2(s - m_new)
    return p, m_new, alpha * l + jnp.sum(p, axis=-1, keepdims=True), alpha


def _softmax_rows(s):
    p = jnp.exp2(s - jnp.max(s, axis=-1, keepdims=True))
    return p, jnp.sum(p, axis=-1, keepdims=True)


def _nsa_attn_kernel(q_ref, gl_ref, rope_ref, qg_ref, kc_ref, vc_ref, ks_ref, vs_ref, kw_ref, vw_ref, o_ref,
                     *, seq, n_sel):
    G, Hg, Dh, QB = NSA_KV_GROUPS, NSA_HEADS_PER_GROUP, NSA_HEAD_DIM, Q_BLOCK
    R = Hg * QB
    s0 = pl.program_id(1) * QB
    ncp = kc_ref.shape[2]
    nb = max(LANES, seq // SEL_BLOCK)
    tk = TK_ATT
    wk = WINDOW + QB

    tab = rope_ref[...]
    tab4 = [jnp.concatenate([tab[k]] * Hg, axis=0) for k in range(3)]
    t_q = s0 + lax.broadcasted_iota(jnp.int32, (QB, 1), 0)
    t_l = s0 + lax.broadcasted_iota(jnp.int32, (1, QB), 1)

    cmp_end = lax.broadcasted_iota(jnp.int32, (1, ncp), 1) * CMP_STRIDE + (CMP_BLOCK - 1)
    bias_c = jnp.where(cmp_end <= t_q, 0.0, NEG)
    row_ok = jnp.where(t_q >= CMP_BLOCK - 1, 1.0, 0.0)
    w0 = pl.multiple_of(jnp.maximum(s0 - WINDOW, 0), QB)
    diff = t_q - (w0 + lax.broadcasted_iota(jnp.int32, (1, wk), 1))
    bias_w = jnp.where((diff >= 0) & (diff < WINDOW), 0.0, NEG)

    j_i = lax.broadcasted_iota(jnp.int32, (nb, ncp), 0) * SEL_BLOCK
    c_i = lax.broadcasted_iota(jnp.int32, (nb, ncp), 1) * CMP_STRIDE
    overlap_t = jnp.where((c_i < j_i + SEL_BLOCK) & (c_i + (CMP_BLOCK - 1) >= j_i), 1.0, 0.0).astype(BF16)
    jt = lax.broadcasted_iota(jnp.int32, (nb, QB), 0)
    cur = jnp.right_shift(t_l, SEL_SHIFT)
    force_t = jnp.where((jt == 0) | (jt == cur) | (jt == cur - 1), SEL_FORCE, 0.0)
    eligible_t = jt <= cur
    jf = jt.astype(F32)

    qbs, sels, o_cs, o_ws = [], [], [], []
    for g in range(G):
        rows = jnp.concatenate([q_ref[:, (g * Hg + h) * Dh:(g * Hg + h + 1) * Dh] for h in range(Hg)], axis=0)
        qb = (_rope_lanes(_rms(rows, qg_ref[...]), tab4, NSA_ROT_DIM // 2) * (Dh ** -0.5 * LOG2E)).astype(BF16)
        qbs.append(qb)

        p, l = _softmax_rows(_dot_nt(qb, kc_ref[0, g]).reshape(Hg, QB, ncp) + bias_c[None])
        p = p * (row_ok[None] / l)
        o_cs.append(jnp.dot(p.reshape(R, ncp).astype(BF16), vc_ref[0, g], preferred_element_type=F32))

        imp = _dot_nt(overlap_t, jnp.sum(p, axis=0).astype(BF16))
        imp = jnp.where(eligible_t, imp + force_t, -jnp.inf)
        sel = jnp.zeros((nb, QB), F32)
        for _ in range(n_sel):
            mx = jnp.max(imp, axis=0, keepdims=True)
            first = jnp.min(jnp.where(imp == mx, jf, float(nb)), axis=0, keepdims=True)
            pick = (jf == first) & (mx > -jnp.inf)
            sel = jnp.where(pick, 1.0, sel)
            imp = jnp.where(pick, -jnp.inf, imp)
        sels.append(sel.T.astype(BF16))

        p, l = _softmax_rows(_dot_nt(qb, kw_ref[0, g, pl.ds(w0, wk), :]).reshape(Hg, QB, wk) + bias_w[None])
        o_w = jnp.dot(p.reshape(R, wk).astype(BF16), vw_ref[0, g, pl.ds(w0, wk), :], preferred_element_type=F32)
        o_ws.append(o_w / l.reshape(R, 1))

    jb = lax.broadcasted_iota(jnp.int32, (nb, tk), 0)
    kl = lax.broadcasted_iota(jnp.int32, (nb, tk), 1)
    kl_row = lax.broadcasted_iota(jnp.int32, (1, tk), 1)

    def sel_step(kt, carry):
        k0 = pl.multiple_of(kt * tk, tk)
        expand = jnp.where(jb == jnp.right_shift(k0 + kl, SEL_SHIFT), 1.0, 0.0).astype(BF16)
        causal = k0 + kl_row <= t_q
        out = []
        for g in range(G):
            m, l, acc = carry[g]
            chosen = jnp.dot(sels[g], expand, preferred_element_type=F32)
            bias = jnp.where((chosen > 0.5) & causal, 0.0, NEG)
            s = _dot_nt(qbs[g], ks_ref[0, g, pl.ds(k0, tk), :]).reshape(Hg, QB, tk) + bias[None]
            p, m, l, alpha = _online_softmax(s, m, l)
            pv = jnp.dot(p.reshape(R, tk).astype(BF16), vs_ref[0, g, pl.ds(k0, tk), :], preferred_element_type=F32)
            out.append((m, l, alpha * acc + pv.reshape(Hg, QB, Dh)))
        return tuple(out)

    init = tuple((jnp.full((Hg, QB, 1), NEG, F32), jnp.zeros((Hg, QB, 1), F32), jnp.zeros((Hg, QB, Dh), F32))
                 for _ in range(G))
    final = lax.fori_loop(0, (s0 + QB + tk - 1) // tk, sel_step, init)

    gates = jax.nn.sigmoid(gl_ref[...])
    for g in range(G):
        o_s = (final[g][2] / final[g][1]).reshape(R, Dh)
        for h in range(Hg):
            c0 = (g * Hg + h) * N_NSA_BRANCHES
            rs = slice(h * QB, (h + 1) * QB)
            o_ref[:, (g * Hg + h) * Dh:(g * Hg + h + 1) * Dh] = (
                gates[:, c0:c0 + 1] * o_cs[g][rs] + gates[:, c0 + 1:c0 + 2] * o_s[rs]
                + gates[:, c0 + 2:c0 + 3] * o_ws[g][rs])


def nsa_attention(proj, rope_n, qg, kc, vc, ks, vs, kw, vw, B, S):
    G, Hg, Dh, QB = NSA_KV_GROUPS, NSA_HEADS_PER_GROUP, NSA_HEAD_DIM, Q_BLOCK
    nq = S // QB
    ncp = kc.shape[2]
    n_sel = min(N_SEL_BLOCKS, S // SEL_BLOCK)
    qcols = G * Hg * Dh
    kv_spec = pl.BlockSpec((1, G, S, Dh), lambda b, i: (b, 0, 0, 0))
    c_spec = pl.BlockSpec((1, G, ncp, Dh), lambda b, i: (b, 0, 0, 0))
    sw = max(TK_ATT, ncp, WINDOW + QB)
    vmem = 2 * G * (4 * S * Dh * 2 + 2 * ncp * Dh * 2) + 8 * G * Hg * QB * sw * 4
    return pl.pallas_call(
        functools.partial(_nsa_attn_kernel, seq=S, n_sel=n_sel),
        grid=(B, nq),
        in_specs=[pl.BlockSpec((QB, qcols), lambda b, i: (b * nq + i, P_Q0 // qcols)),
                  pl.BlockSpec((QB, LANES), lambda b, i: (b * nq + i, P_G0 // LANES)),
                  pl.BlockSpec((3, QB, Dh), lambda b, i: (0, b * nq + i, 0)),
                  pl.BlockSpec((1, Dh), lambda b, i: (0, 0)),
                  c_spec, c_spec, kv_spec, kv_spec, kv_spec, kv_spec],
        out_specs=pl.BlockSpec((QB, qcols), lambda b, i: (b * nq + i, 0)),
        out_shape=jax.ShapeDtypeStruct((B * S, qcols), F32),
        compiler_params=_cparams(("parallel", "arbitrary"), vmem),
        name="nsa_attention",
    )(proj, proj, rope_n, qg, kc, vc, ks, vs, kw, vw)


def _mla_q_kernel(qa_ref, ga_ref, w_ref, gq_ref, rope_ref, o_ref):
    W = MLA_QK_PAD
    c = _rms(qa_ref[...], ga_ref[...]).astype(BF16)
    q = jnp.dot(c, w_ref[...], preferred_element_type=F32)
    tab = rope_ref[...]
    fold = MLA_QK_DIM ** -0.5 * LOG2E
    for h in range(MLA_HEADS):
        qh = q[:, h * W:(h + 1) * W]
        ms = jnp.sum(qh * qh, axis=-1, keepdims=True) * (1.0 / MLA_QK_DIM)
        qn = qh * lax.rsqrt(ms + NORM_EPS) * gq_ref[...]
        o_ref[0, h, :, :LANES] = (qn[:, :LANES] * fold).astype(BF16)
        o_ref[0, h, :, LANES:] = (_rope_lanes(qn[:, LANES:], tab, MLA_ROPE_DIM // 2) * fold).astype(BF16)


def mla_q_prep(proj, ga, wq, gq, rope_m, B, S):
    tm = min(TM, S)
    ns = S // tm
    H, R, W = MLA_HEADS, MLA_Q_RANK, MLA_QK_PAD
    return pl.pallas_call(
        _mla_q_kernel,
        grid=(B * ns,),
        in_specs=[pl.BlockSpec((tm, R), lambda r: (r, P_QA0 // R)),
                  pl.BlockSpec((1, R), lambda r: (0, 0)),
                  pl.BlockSpec((R, H * W), lambda r: (0, 0)),
                  pl.BlockSpec((1, W), lambda r: (0, 0)),
                  pl.BlockSpec((3, tm, LANES), lambda r: (0, r, 0))],
        out_specs=pl.BlockSpec((1, H, tm, W), lambda r: (r // ns, 0, r % ns, 0)),
        out_shape=jax.ShapeDtypeStruct((B, H, S, W), BF16),
        compiler_params=_cparams(("parallel",), 2 * (tm * R * 4 + R * H * W * 2 + tm * H * W * 2) + 3 * tm * H * W * 4),
        name="mla_q_prep",
    )(proj, ga, wq, gq, rope_m)


def _mla_kv_kernel(kva_ref, kr_ref, ga_ref, w_ref, gk_ref, rope_ref, k_ref, v_ref):
    W = MLA_QK_PAD
    c = _rms(kva_ref[...], ga_ref[...]).astype(BF16)
    kv = jnp.dot(c, w_ref[...], preferred_element_type=F32)
    tab = rope_ref[...]
    k_rot = kr_ref[...]
    ss_rot = jnp.sum(k_rot * k_rot, axis=-1, keepdims=True)
    for h in range(MLA_HEADS):
        k_nope = kv[:, h * W:h * W + LANES]
        ms = (jnp.sum(k_nope * k_nope, axis=-1, keepdims=True) + ss_rot) * (1.0 / MLA_QK_DIM)
        r = lax.rsqrt(ms + NORM_EPS)
        k_ref[0, h, :, :LANES] = (k_nope * r * gk_ref[:, :LANES]).astype(BF16)
        k_ref[0, h, :, LANES:] = _rope_lanes(k_rot * r * gk_ref[:, LANES:], tab, MLA_ROPE_DIM // 2).astype(BF16)
        v_ref[0, h] = kv[:, h * W + LANES:(h + 1) * W].astype(BF16)


def mla_kv_prep(proj, ga, wkv, gk, rope_m, B, S):
    tm = min(TM, S)
    ns = S // tm
    H, R, W = MLA_HEADS, MLA_KV_RANK, MLA_QK_PAD
    return pl.pallas_call(
        _mla_kv_kernel,
        grid=(B * ns,),
        in_specs=[pl.BlockSpec((tm, R), lambda r: (r, P_KVA0 // R)),
                  pl.BlockSpec((tm, LANES), lambda r: (r, P_KR0 // LANES)),
                  pl.BlockSpec((1, R), lambda r: (0, 0)),
                  pl.BlockSpec((R, H * W), lambda r: (0, 0)),
                  pl.BlockSpec((1, W), lambda r: (0, 0)),
                  pl.BlockSpec((3, tm, LANES), lambda r: (0, r, 0))],
        out_specs=[pl.BlockSpec((1, H, tm, W), lambda r: (r // ns, 0, r % ns, 0)),
                   pl.BlockSpec((1, H, tm, MLA_V_DIM), lambda r: (r // ns, 0, r % ns, 0))],
        out_shape=[jax.ShapeDtypeStruct((B, H, S, W), BF16), jax.ShapeDtypeStruct((B, H, S, MLA_V_DIM), BF16)],
        compiler_params=_cparams(("parallel",), 2 * (tm * R * 4 + R * H * W * 2 + tm * H * W * 3) + 3 * tm * H * W * 4),
        name="mla_kv_prep",
    )(proj, proj, ga, wkv, gk, rope_m)


def _mla_attn_kernel(q_ref, k_ref, v_ref, o_ref, *, tq, heads):
    i = pl.program_id(2)
    Dv = v_ref.shape[-1]

    def tile(k0, carry, bias):
        out = []
        for c in range(heads):
            m, l, acc = carry[c]
            s = _dot_nt(q_ref[0, c], k_ref[0, c, pl.ds(k0, tq), :])
            if bias is not None:
                s = s + bias
            p, m, l, alpha = _online_softmax(s, m, l)
            pv = jnp.dot(p.astype(BF16), v_ref[0, c, pl.ds(k0, tq), :], preferred_element_type=F32)
            out.append((m, l, alpha * acc + pv))
        return tuple(out)

    init = tuple((jnp.full((tq, 1), NEG, F32), jnp.zeros((tq, 1), F32), jnp.zeros((tq, Dv), F32))
                 for _ in range(heads))
    carry = lax.fori_loop(0, i, lambda kt, c: tile(pl.multiple_of(kt * tq, tq), c, None), init)
    causal = jnp.where(lax.broadcasted_iota(jnp.int32, (tq, tq), 1) <= lax.broadcasted_iota(jnp.int32, (tq, tq), 0),
                       0.0, NEG)
    carry = tile(pl.multiple_of(i * tq, tq), carry, causal)
    for c in range(heads):
        o_ref[:, c * Dv:(c + 1) * Dv] = carry[c][2] / carry[c][1]


def mla_attention(q, k, v):
    B, H, S, W = q.shape
    tq = min(TQ_MLA, S)
    nq = S // tq
    Dv = v.shape[-1]
    hp = MLA_HEADS_PER_STEP
    vmem = hp * (2 * (S * W * 2 + S * Dv * 2) + 4 * tq * W * 2 + 8 * tq * tq * 4)
    return pl.pallas_call(
        functools.partial(_mla_attn_kernel, tq=tq, heads=hp),
        grid=(B, H // hp, nq),
        in_specs=[pl.BlockSpec((1, hp, tq, W), lambda b, h, i: (b, h, i, 0)),
                  pl.BlockSpec((1, hp, S, W), lambda b, h, i: (b, h, 0, 0)),
                  pl.BlockSpec((1, hp, S, Dv), lambda b, h, i: (b, h, 0, 0))],
        out_specs=pl.BlockSpec((tq, hp * Dv), lambda b, h, i: (b * nq + i, h)),
        out_shape=jax.ShapeDtypeStruct((B * S, H * Dv), F32),
        compiler_params=_cparams(("parallel", "parallel", "arbitrary"), vmem),
        name="mla_attention",
    )(q, k, v)


def _out_proj_kernel(x_ref, a_ref, b_ref, ga_ref, gb_ref, wa_ref, wb_ref, o_ref, na_ref, nb_ref):
    @pl.when(pl.program_id(1) == 0)
    def _():
        na_ref[...] = _rms(a_ref[...], ga_ref[...]).astype(BF16)
        nb_ref[...] = _rms(b_ref[...], gb_ref[...]).astype(BF16)

    o_ref[...] = (x_ref[...] + jnp.dot(na_ref[...], wa_ref[...], preferred_element_type=F32)
                  + jnp.dot(nb_ref[...], wb_ref[...], preferred_element_type=F32))


def out_proj(x, o_nsa, o_mla, g_nsa, g_mla, w_out):
    T, D = x.shape
    Ka, Kb = o_nsa.shape[1], o_mla.shape[1]
    assert Ka == Kb
    tm, tn = min(TM, T), TN_OUT
    vmem = 2 * (tm * tn * 8 + 2 * tm * Ka * 4 + 2 * Ka * tn * 2) + 2 * tm * Ka * 2
    return pl.pallas_call(
        _out_proj_kernel,
        grid=(T // tm, D // tn),
        in_specs=[pl.BlockSpec((tm, tn), lambda i, j: (i, j)),
                  pl.BlockSpec((tm, Ka), lambda i, j: (i, 0)),
                  pl.BlockSpec((tm, Kb), lambda i, j: (i, 0)),
                  pl.BlockSpec((1, Ka), lambda i, j: (0, 0)),
                  pl.BlockSpec((1, Kb), lambda i, j: (0, 0)),
                  pl.BlockSpec((Ka, tn), lambda i, j: (0, j)),
                  pl.BlockSpec((Kb, tn), lambda i, j: (1, j))],
        out_specs=pl.BlockSpec((tm, tn), lambda i, j: (i, j)),
        out_shape=jax.ShapeDtypeStruct((T, D), F32),
        scratch_shapes=[pltpu.VMEM((tm, Ka), BF16), pltpu.VMEM((tm, Kb), BF16)],
        compiler_params=_cparams(("parallel", "arbitrary"), 2 * vmem),
        name="out_proj",
    )(x, o_nsa, o_mla, g_nsa, g_mla, w_out, w_out)


def _swiglu_kernel(be_ref, x_ref, g_ref, wg_ref, wu_ref, wd_ref, o_ref, h_ref, acc_ref, *, residual):
    i = pl.program_id(0)
    f = pl.program_id(1)
    live = i < be_ref[0]

    @pl.when(live & (f == 0))
    def _():
        h_ref[...] = _rms(x_ref[...], g_ref[...]).astype(BF16)
        acc_ref[...] = jnp.zeros_like(acc_ref)

    @pl.when(live)
    def _():
        h = h_ref[...]
        a = jax.nn.silu(jnp.dot(h, wg_ref[0], preferred_element_type=F32)) * jnp.dot(
            h, wu_ref[0], preferred_element_type=F32)
        acc_ref[...] += jnp.dot(a.astype(BF16), wd_ref[0], preferred_element_type=F32)

    last = f == pl.num_programs(1) - 1

    @pl.when(live & last)
    def _():
        o_ref[...] = x_ref[...] + acc_ref[...] if residual else acc_ref[...]

    @pl.when(jnp.logical_not(live) & last)
    def _():
        o_ref[...] = jnp.zeros_like(o_ref)


def swiglu_blocks(x, g, wg, wu, wd, block_info, tm, residual):
    R, D = x.shape
    F = wg.shape[2]
    tf = TF
    nf = F // tf

    def wcol(i, f, be):
        live = i < be[0]
        return (be[1 + i], 0, jnp.where(live, f, nf - 1))

    def wrow(i, f, be):
        live = i < be[0]
        return (be[1 + i], jnp.where(live, f, nf - 1), 0)

    vmem = 4 * tm * D * 4 + tm * D * 2 + tm * D * 4 + 2 * 3 * D * tf * 2 + 6 * tm * tf * 4
    grid_spec = pltpu.PrefetchScalarGridSpec(
        num_scalar_prefetch=1,
        grid=(R // tm, nf),
        in_specs=[pl.BlockSpec((tm, D), lambda i, f, be: (i, 0)),
                  pl.BlockSpec((1, D), lambda i, f, be: (0, 0)),
                  pl.BlockSpec((1, D, tf), wcol),
                  pl.BlockSpec((1, D, tf), wcol),
                  pl.BlockSpec((1, tf, D), wrow)],
        out_specs=pl.BlockSpec((tm, D), lambda i, f, be: (i, 0)),
        scratch_shapes=[pltpu.VMEM((tm, D), BF16), pltpu.VMEM((tm, D), F32)],
    )
    return pl.pallas_call(
        functools.partial(_swiglu_kernel, residual=residual),
        grid_spec=grid_spec,
        out_shape=jax.ShapeDtypeStruct((R, D), F32),
        compiler_params=_cparams(("parallel", "arbitrary"), vmem + (4 << 20)),
        name="swiglu",
    )(block_info, x, g, wg, wu, wd)


def _router_kernel(x_ref, g_ref, w_ref, o_ref):
    h = _rms(x_ref[...], g_ref[...]).astype(BF16)
    logits = jnp.dot(h, w_ref[...], preferred_element_type=F32)
    lane = lax.broadcasted_iota(jnp.int32, logits.shape, 1)
    lf = lane.astype(F32)
    lg = jnp.where(lane < N_EXPERTS, logits, -jnp.inf)
    m1 = jnp.max(lg, axis=-1, keepdims=True)
    i1 = jnp.min(jnp.where(lg == m1, lf, float(LANES)), axis=-1, keepdims=True)
    lg2 = jnp.where(lf == i1, -jnp.inf, lg)
    m2 = jnp.max(lg2, axis=-1, keepdims=True)
    i2 = jnp.min(jnp.where(lg2 == m2, lf, float(LANES)), axis=-1, keepdims=True)
    e2 = jnp.exp(m2 - m1)
    den = 1.0 + e2
    o_ref[...] = jnp.where(lane == 0, i1, jnp.where(lane == 1, i2, jnp.where(lane == 2, 1.0 / den, e2 / den)))


def moe_router(x, g, w_router_p):
    T, D = x.shape
    tm = min(TM, T)
    return pl.pallas_call(
        _router_kernel,
        grid=(T // tm,),
        in_specs=[pl.BlockSpec((tm, D), lambda i: (i, 0)),
                  pl.BlockSpec((1, D), lambda i: (0, 0)),
                  pl.BlockSpec((D, LANES), lambda i: (0, 0))],
        out_specs=pl.BlockSpec((tm, LANES), lambda i: (i, 0)),
        out_shape=jax.ShapeDtypeStruct((T, LANES), F32),
        compiler_params=_cparams(("parallel",), 4 * tm * D * 4),
        name="moe_router",
    )(x, g, w_router_p)


def _gather_kernel(idx_ref, src_ref, dst_ref, sem):
    base = pl.program_id(0) * GATHER_ROWS

    def row_copy(r):
        return pltpu.make_async_copy(src_ref.at[idx_ref[base + r]], dst_ref.at[r], sem)

    def start(r, c):
        row_copy(r).start()
        return c

    def wait(r, c):
        row_copy(r).wait()
        return c

    lax.fori_loop(0, GATHER_ROWS, start, 0)
    lax.fori_loop(0, GATHER_ROWS, wait, 0)


def gather_rows(src, idx):
    n = idx.shape[0]
    D = src.shape[1]
    src = src.reshape(src.shape[0], D // LANES, LANES)
    grid_spec = pltpu.PrefetchScalarGridSpec(
        num_scalar_prefetch=1,
        grid=(n // GATHER_ROWS,),
        in_specs=[pl.BlockSpec(memory_space=pl.ANY)],
        out_specs=pl.BlockSpec((GATHER_ROWS, D // LANES, LANES), lambda i, idx: (i, 0, 0)),
        scratch_shapes=[pltpu.SemaphoreType.DMA(())],
    )
    return pl.pallas_call(
        _gather_kernel,
        grid_spec=grid_spec,
        out_shape=jax.ShapeDtypeStruct((n, D // LANES, LANES), src.dtype),
        compiler_params=_cparams(("arbitrary",), 4 * GATHER_ROWS * D * 4),
        name="gather_rows",
    )(idx, src).reshape(n, D)


def _combine_kernel(x_ref, ya_ref, yb_ref, gate_ref, o_ref):
    gate = gate_ref[...]
    o_ref[...] = x_ref[...] + (ya_ref[...] * gate[:, 2:3] + yb_ref[...] * gate[:, 3:4])


def moe_combine(x, y_pair, route):
    T, D = x.shape
    tm = min(TM, T)
    nt = T // tm
    return pl.pallas_call(
        _combine_kernel,
        grid=(nt,),
        in_specs=[pl.BlockSpec((tm, D), lambda i: (i, 0)),
                  pl.BlockSpec((tm, D), lambda i: (i, 0)),
                  pl.BlockSpec((tm, D), lambda i: (i + nt, 0)),
                  pl.BlockSpec((tm, LANES), lambda i: (i, 0))],
        out_specs=pl.BlockSpec((tm, D), lambda i: (i, 0)),
        out_shape=jax.ShapeDtypeStruct((T, D), F32),
        compiler_params=_cparams(("parallel",), 10 * tm * D * 4),
        name="moe_combine",
    )(x, y_pair, y_pair, route)


def moe_ffn(x, g, w_router_p, wg, wu, wd, expert_base):
    T, D = x.shape
    n_assign = T * TOP_K
    route = moe_router(x, g, w_router_p)
    flat_e = route[:, :TOP_K].astype(jnp.int32).reshape(-1)
    order = jnp.argsort(flat_e)
    se = flat_e[order]
    stok = (order // TOP_K).astype(jnp.int32)
    counts = jnp.bincount(flat_e, length=N_EXPERTS)
    padded = (counts + MOE_TM - 1) // MOE_TM * MOE_TM
    pad_end = jnp.cumsum(padded)
    pad_start = pad_end - padded
    start = jnp.cumsum(counts) - counts
    dest = (pad_start[se] + jnp.arange(n_assign) - start[se]).astype(jnp.int32)
    n_blocks = -(-(n_assign + N_EXPERTS * (MOE_TM - 1)) // MOE_TM)
    n_rows = n_blocks * MOE_TM
    row_tok = jnp.zeros((n_rows,), jnp.int32).at[dest].set(stok)
    block_e = jnp.minimum(jnp.searchsorted(pad_end, jnp.arange(n_blocks) * MOE_TM, side='right'), N_EXPERTS - 1)
    n_live = (pad_end[-1] // MOE_TM).astype(jnp.int32)
    block_info = jnp.concatenate([n_live[None], block_e.astype(jnp.int32) + expert_base])
    dest_flat = jnp.zeros((n_assign,), jnp.int32).at[order].set(dest)
    pair_idx = jnp.concatenate([dest_flat[0::2], dest_flat[1::2]])

    x_rows = gather_rows(x, row_tok)
    y_rows = swiglu_blocks(x_rows, g, wg, wu, wd, block_info, MOE_TM, residual=False)
    y_pair = gather_rows(y_rows, pair_idx)
    return moe_combine(x, y_pair, route)


def _rope_tab(pos, rot_dim):
    half = rot_dim // 2
    inv_freq = jnp.power(jnp.float32(ROPE_THETA), -jnp.arange(0, rot_dim, 2, dtype=F32) / rot_dim)
    ang = pos.astype(F32)[..., None] * inv_freq
    cos, sin = jnp.cos(ang), jnp.sin(ang)
    rest = LANES - rot_dim
    one = jnp.ones(pos.shape + (rest,), F32)
    zr = jnp.zeros(pos.shape + (rest,), F32)
    zh = jnp.zeros_like(sin)
    return jnp.stack([jnp.concatenate([cos, cos, one], -1),
                      jnp.concatenate([zh, sin, zr], -1),
                      jnp.concatenate([-sin, zh, zr], -1)])


def _pad_cols(w, n):
    return jnp.pad(w, ((0, 0), (0, n - w.shape[1])))


def kernel(x, positions, attn_norm, w_in, nsa_q_norm, nsa_k_norm, cmp_k_w1, cmp_k_w2, cmp_k_pos, cmp_v_w1, cmp_v_w2, cmp_v_pos, mla_q_a_norm, mla_w_q_up, mla_kv_a_norm, mla_w_kv_up, mla_q_norm, mla_k_norm, nsa_out_norm, mla_out_norm, w_out, ffn_norm, dense_w_gate, dense_w_up, dense_w_down, moe_router, moe_w_gate, moe_w_up, moe_w_down):
    B, S, D = x.shape
    T = B * S
    depth = w_in.shape[0]
    Dh = NSA_HEAD_DIM
    ncp = S // CMP_STRIDE
    kw = CMP_STRIDE * Dh

    rope_n = _rope_tab(positions, NSA_ROT_DIM).reshape(3, T, LANES)
    rope_m = _rope_tab(positions, MLA_ROPE_DIM).reshape(3, T, LANES)
    cmp_idx = jnp.minimum(jnp.arange(ncp) * CMP_STRIDE + CMP_BLOCK - 1, S - 1)
    rope_c = _rope_tab(positions[:, cmp_idx], NSA_ROT_DIM)

    dense_g, dense_u, dense_d = (w.astype(BF16) for w in (dense_w_gate, dense_w_up, dense_w_down))
    moe_g, moe_u, moe_d = (w.reshape((-1,) + w.shape[2:]).astype(BF16) for w in (moe_w_gate, moe_w_up, moe_w_down))

    g0 = NSA_Q_COLS + NSA_KV_COLS
    m0 = g0 + NSA_GATE_COLS
    xt = x.reshape(T, D)
    for layer in range(depth):
        w = w_in[layer]
        w_in_p = jnp.concatenate([w[:, :g0], w[:, m0:m0 + MLA_Q_RANK], w[:, m0 + MLA_Q_RANK:m0 + MLA_Q_RANK + MLA_KV_RANK],
                                  _pad_cols(w[:, m0 + MLA_Q_RANK + MLA_KV_RANK:], LANES),
                                  _pad_cols(w[:, g0:m0], LANES)], axis=1).astype(BF16)
        proj = rms_matmul(xt, attn_norm[layer][None], w_in_p, min(TM, T), TN_PROJ)

        gk = nsa_k_norm[layer]
        kc_raw, vc_raw, ks, vs, kwin, vwin = nsa_prep(proj, rope_n, gk, B, S)
        kc, vc = nsa_compress(
            kc_raw, vc_raw,
            cmp_k_w1[layer].reshape(2, kw, Dh).astype(BF16), cmp_k_w2[layer].astype(BF16),
            cmp_k_pos[layer].reshape(2, 1, kw),
            cmp_v_w1[layer].reshape(2, kw, Dh).astype(BF16), cmp_v_w2[layer].astype(BF16),
            cmp_v_pos[layer].reshape(2, 1, kw), rope_c, gk)
        o_nsa = nsa_attention(proj, rope_n, nsa_q_norm[layer][None], kc, vc, ks, vs, kwin, vwin, B, S)

        wq = jnp.pad(mla_w_q_up[layer].reshape(MLA_Q_RANK, MLA_HEADS, MLA_QK_DIM),
                     ((0, 0), (0, 0), (0, MLA_QK_PAD - MLA_QK_DIM))).reshape(MLA_Q_RANK, MLA_HEADS * MLA_QK_PAD)
        gq = jnp.pad(mla_q_norm[layer], (0, MLA_QK_PAD - MLA_QK_DIM))[None]
        gkm = jnp.pad(mla_k_norm[layer], (0, MLA_QK_PAD - MLA_QK_DIM))[None]
        q_m = mla_q_prep(proj, mla_q_a_norm[layer][None], wq.astype(BF16), gq, rope_m, B, S)
        k_m, v_m = mla_kv_prep(proj, mla_kv_a_norm[layer][None], mla_w_kv_up[layer].astype(BF16), gkm, rope_m, B, S)
        o_mla = mla_attention(q_m, k_m, v_m)

        xt = out_proj(xt, o_nsa, o_mla, nsa_out_norm[layer][None], mla_out_norm[layer][None],
                      w_out[layer].astype(BF16))

        i = layer // 2
        gf = ffn_norm[layer][None]
        if layer % 2 == 0:
            tm = min(TM, T)
            info = jnp.concatenate([jnp.full((1,), T // tm, jnp.int32), jnp.full((T // tm,), i, jnp.int32)])
            xt = swiglu_blocks(xt, gf, dense_g, dense_u, dense_d, info, tm, residual=True)
        else:
            xt = moe_ffn(xt, gf, _pad_cols(moe_router[i], LANES).astype(BF16), moe_g, moe_u, moe_d, i * N_EXPERTS)
    return xt.reshape(B, S, D)
```

```python
import functools

import jax
import jax.numpy as jnp
from jax import lax
from jax.experimental import pallas as pl
from jax.experimental.pallas import tpu as pltpu

F32 = jnp.float32
BF16 = jnp.bfloat16

LANES = 128
VMEM_CAP_BYTES = 56 * 1024 * 1024

LOG2E = 1.4426950408889634
NEG = -1e30

NORM_EPS = 1e-6
ROPE_THETA = 500000.0
Q_BLOCK = 128

NSA_HEAD_DIM = 128
NSA_KV_GROUPS = 2
NSA_HEADS_PER_GROUP = 4
NSA_HEADS = NSA_KV_GROUPS * NSA_HEADS_PER_GROUP
NSA_ROT_DIM = NSA_HEAD_DIM // 4
N_NSA_BRANCHES = 3
CMP_BLOCK = 32
CMP_STRIDE = 16
SEL_BLOCK = 64
SEL_SHIFT = SEL_BLOCK.bit_length() - 1
assert 1 << SEL_SHIFT == SEL_BLOCK
N_SEL_BLOCKS = 16
WINDOW = 512
SEL_FORCE = 1.0e4

MLA_V_DIM = 128
MLA_HEADS = 8
MLA_Q_RANK = 512
MLA_KV_RANK = 512
MLA_NOPE_DIM = 128
MLA_ROPE_DIM = 64
MLA_QK_DIM = MLA_NOPE_DIM + MLA_ROPE_DIM
MLA_QK_PAD = 256

N_EXPERTS = 8
TOP_K = 2

NSA_Q_COLS = NSA_HEADS * NSA_HEAD_DIM
NSA_KV_COLS = N_NSA_BRANCHES * 2 * NSA_KV_GROUPS * NSA_HEAD_DIM
NSA_GATE_COLS = NSA_HEADS * N_NSA_BRANCHES
P_Q0 = 0
P_KV0 = P_Q0 + NSA_Q_COLS
P_QA0 = P_KV0 + NSA_KV_COLS
P_KVA0 = P_QA0 + MLA_Q_RANK
P_KR0 = P_KVA0 + MLA_KV_RANK
P_G0 = P_KR0 + LANES
P_COLS = P_G0 + LANES

TM = 512
TN_PROJ = 768
TN_OUT = 512
TF = 512
TS_PREP = 1024
TK_ATT = 512
TQ_MLA = 512
MLA_HEADS_PER_STEP = 4
MOE_TM = 512
GATHER_ROWS = 512
GATHER_UNROLL = 8


def _cparams(sem, vmem_bytes):
    return pltpu.CompilerParams(dimension_semantics=sem,
                                vmem_limit_bytes=int(min(VMEM_CAP_BYTES, max(vmem_bytes, 16 * 1024 * 1024))))


def _rms(x, g):
    ms = jnp.mean(x * x, axis=-1, keepdims=True)
    return x * lax.rsqrt(ms + NORM_EPS) * g


def _rope_lanes(x, tab, half):
    return x * tab[0] + pltpu.roll(x, half, 1) * tab[1] + pltpu.roll(x, LANES - half, 1) * tab[2]


def _dot_nt(a, b):
    return lax.dot_general(a, b, (((1,), (1,)), ((), ())), preferred_element_type=F32)


def _rms_matmul_kernel(x_ref, g_ref, w_ref, o_ref, h_ref):
    @pl.when(pl.program_id(1) == 0)
    def _():
        h_ref[...] = _rms(x_ref[...], g_ref[...]).astype(BF16)

    o_ref[...] = jnp.dot(h_ref[...], w_ref[...], preferred_element_type=F32)


def rms_matmul(x, g, w, tm, tn):
    T, K = x.shape
    N = w.shape[1]
    vmem = 2 * tm * K * 4 + tm * K * 2 + 2 * K * tn * 2 + 2 * tm * tn * 4
    return pl.pallas_call(
        _rms_matmul_kernel,
        grid=(T // tm, N // tn),
        in_specs=[pl.BlockSpec((tm, K), lambda i, j: (i, 0)),
                  pl.BlockSpec((1, K), lambda i, j: (0, 0)),
                  pl.BlockSpec((K, tn), lambda i, j: (0, j))],
        out_specs=pl.BlockSpec((tm, tn), lambda i, j: (i, j)),
        out_shape=jax.ShapeDtypeStruct((T, N), F32),
        scratch_shapes=[pltpu.VMEM((tm, K), BF16)],
        compiler_params=_cparams(("parallel", "arbitrary"), 2 * vmem),
        name="rms_matmul",
    )(x, g, w)


def _nsa_prep_kernel(q_in, kc_in, vc_in, ks_in, vs_in, kw_in, vw_in, rope_ref, gq_ref, gk_ref,
                     q_out, kc_out, vc_out, ks_out, vs_out, kw_out, vw_out):
    tab = rope_ref[...]
    half = NSA_ROT_DIM // 2
    Dh = NSA_HEAD_DIM
    fold = Dh ** -0.5 * LOG2E
    for h in range(NSA_HEADS_PER_GROUP):
        qh = _rope_lanes(_rms(q_in[:, h * Dh:(h + 1) * Dh], gq_ref[...]), tab, half)
        q_out[0, 0, h] = (qh * fold).astype(BF16)
    kc_out[0, 0] = kc_in[...]
    vc_out[0, 0] = vc_in[...]
    ks_out[0, 0] = _rope_lanes(_rms(ks_in[...], gk_ref[1:2, :]), tab, half).astype(BF16)
    kw_out[0, 0] = _rope_lanes(_rms(kw_in[...], gk_ref[2:3, :]), tab, half).astype(BF16)
    vs_out[0, 0] = vs_in[...].astype(BF16)
    vw_out[0, 0] = vw_in[...].astype(BF16)


def nsa_prep(proj, rope_n, gq, gk, B, S):
    ts = min(TS_PREP, S)
    ns = S // ts
    G, Hg, Dh = NSA_KV_GROUPS, NSA_HEADS_PER_GROUP, NSA_HEAD_DIM
    kv_blk0 = P_KV0 // Dh

    def in_spec(branch, kv):
        off = kv_blk0 + (branch * 2 + kv) * G
        return pl.BlockSpec((ts, Dh), lambda b, s, g: (b * ns + s, off + g))

    out_spec = pl.BlockSpec((1, 1, ts, Dh), lambda b, s, g: (b, g, s, 0))
    f32_out = jax.ShapeDtypeStruct((B, G, S, Dh), F32)
    bf_out = jax.ShapeDtypeStruct((B, G, S, Dh), BF16)
    return pl.pallas_call(
        _nsa_prep_kernel,
        grid=(B, ns, G),
        in_specs=[pl.BlockSpec((ts, Hg * Dh), lambda b, s, g: (b * ns + s, P_Q0 // (Hg * Dh) + g)),
                  in_spec(0, 0), in_spec(0, 1), in_spec(1, 0), in_spec(1, 1), in_spec(2, 0), in_spec(2, 1),
                  pl.BlockSpec((3, ts, Dh), lambda b, s, g: (0, b * ns + s, 0)),
                  pl.BlockSpec((1, Dh), lambda b, s, g: (0, 0)),
                  pl.BlockSpec((N_NSA_BRANCHES, Dh), lambda b, s, g: (0, 0))],
        out_specs=[pl.BlockSpec((1, 1, Hg, ts, Dh), lambda b, s, g: (b, g, 0, s, 0))] + [out_spec] * 6,
        out_shape=[jax.ShapeDtypeStruct((B, G, Hg, S, Dh), BF16), f32_out, f32_out, bf_out, bf_out, bf_out, bf_out],
        compiler_params=_cparams(("parallel", "parallel", "parallel"), 2 * 2 * 20 * ts * Dh * 4),
        name="nsa_prep",
    )(proj, proj, proj, proj, proj, proj, proj, rope_n, gq, gk)


def _compress_one(t_ref, w1_ref, w2_ref, pos_ref):
    t = t_ref[0, 0]
    ncp = t.shape[0]
    u = jnp.dot((t + pos_ref[0]).astype(BF16), w1_ref[0], preferred_element_type=F32)
    v = jnp.dot((t + pos_ref[1]).astype(BF16), w1_ref[1], preferred_element_type=F32)
    pre = u + pltpu.roll(v, ncp - 1, 0)
    return jnp.dot(jax.nn.gelu(pre).astype(BF16), w2_ref[...], preferred_element_type=F32)


def _compress_kernel(tk_ref, tv_ref, w1k_ref, w2k_ref, posk_ref, w1v_ref, w2v_ref, posv_ref, rope_ref, gk_ref,
                     kc_ref, vc_ref):
    k = _compress_one(tk_ref, w1k_ref, w2k_ref, posk_ref)
    kc_ref[0, 0] = _rope_lanes(_rms(k, gk_ref[0:1, :]), rope_ref[:, 0], NSA_ROT_DIM // 2).astype(BF16)
    vc_ref[0, 0] = _compress_one(tv_ref, w1v_ref, w2v_ref, posv_ref).astype(BF16)


def nsa_compress(kc_raw, vc_raw, w1k, w2k, posk, w1v, w2v, posv, rope_c, gk):
    B, G, S, Dh = kc_raw.shape
    ncp = S // CMP_STRIDE
    kw = CMP_STRIDE * Dh
    tk = kc_raw.reshape(B, G, ncp, kw)
    tv = vc_raw.reshape(B, G, ncp, kw)
    t_spec = pl.BlockSpec((1, 1, ncp, kw), lambda b, g: (b, g, 0, 0))
    w1_spec = pl.BlockSpec((2, kw, Dh), lambda b, g: (0, 0, 0))
    w2_spec = pl.BlockSpec((Dh, Dh), lambda b, g: (0, 0))
    pos_spec = pl.BlockSpec((2, 1, kw), lambda b, g: (0, 0, 0))
    out_spec = pl.BlockSpec((1, 1, ncp, Dh), lambda b, g: (b, g, 0, 0))
    out = jax.ShapeDtypeStruct((B, G, ncp, Dh), BF16)
    return pl.pallas_call(
        _compress_kernel,
        grid=(B, G),
        in_specs=[t_spec, t_spec, w1_spec, w2_spec, pos_spec, w1_spec, w2_spec, pos_spec,
                  pl.BlockSpec((3, 1, ncp, Dh), lambda b, g: (0, b, 0, 0)),
                  pl.BlockSpec((N_NSA_BRANCHES, Dh), lambda b, g: (0, 0))],
        out_specs=[out_spec, out_spec],
        out_shape=[out, out],
        compiler_params=_cparams(("parallel", "parallel"), 2 * (4 * ncp * kw * 4 + 8 * kw * Dh * 2)),
        name="nsa_compress",
    )(tk, tv, w1k, w2k, posk, w1v, w2v, posv, rope_c, gk)


def _online_softmax(s, m, l):
    m_new = jnp.maximum(m, jnp.max(s, axis=-1, keepdims=True))
    alpha = jnp.exp2(m - m_new)
    p = jnp.exp2(s - m_new)
    return p, m_new, alpha * l + jnp.sum(p, axis=-1, keepdims=True), alpha


def _softmax_rows(s):
    p = jnp.exp2(s - jnp.max(s, axis=-1, keepdims=True))
    return p, jnp.sum(p, axis=-1, keepdims=True)


def _nsa_attn_kernel(q_ref, gl_ref, kc_ref, vc_ref, ks_ref, vs_ref, kw_ref, vw_ref, o_ref,
                     *, seq, n_sel):
    G, Hg, Dh, QB = NSA_KV_GROUPS, NSA_HEADS_PER_GROUP, NSA_HEAD_DIM, Q_BLOCK
    R = Hg * QB
    s0 = pl.program_id(1) * QB
    ncp = kc_ref.shape[2]
    nb = max(LANES, seq // SEL_BLOCK)
    tk = TK_ATT
    wk = WINDOW + QB

    t_q = s0 + lax.broadcasted_iota(jnp.int32, (QB, 1), 0)
    t_l = s0 + lax.broadcasted_iota(jnp.int32, (1, QB), 1)

    cmp_end = lax.broadcasted_iota(jnp.int32, (1, ncp), 1) * CMP_STRIDE + (CMP_BLOCK - 1)
    bias_c = jnp.where(cmp_end <= t_q, 0.0, NEG)
    row_ok = jnp.where(t_q >= CMP_BLOCK - 1, 1.0, 0.0)
    w0 = pl.multiple_of(jnp.maximum(s0 - WINDOW, 0), QB)
    diff = t_q - (w0 + lax.broadcasted_iota(jnp.int32, (1, wk), 1))
    bias_w = jnp.where((diff >= 0) & (diff < WINDOW), 0.0, NEG)

    j_i = lax.broadcasted_iota(jnp.int32, (nb, ncp), 0) * SEL_BLOCK
    c_i = lax.broadcasted_iota(jnp.int32, (nb, ncp), 1) * CMP_STRIDE
    overlap_t = jnp.where((c_i < j_i + SEL_BLOCK) & (c_i + (CMP_BLOCK - 1) >= j_i), 1.0, 0.0).astype(BF16)
    jt = lax.broadcasted_iota(jnp.int32, (nb, QB), 0)
    cur = jnp.right_shift(t_l, SEL_SHIFT)
    force_t = jnp.where((jt == 0) | (jt == cur) | (jt == cur - 1), SEL_FORCE, 0.0)
    eligible_t = jt <= cur
    jf = jt.astype(F32)

    qbs, sels, o_cs, o_ws = [], [], [], []
    for g in range(G):
        qb = q_ref[0, g].reshape(R, Dh)
        qbs.append(qb)

        p, l = _softmax_rows(_dot_nt(qb, kc_ref[0, g]).reshape(Hg, QB, ncp) + bias_c[None])
        p = p * (row_ok[None] / l)
        o_cs.append(jnp.dot(p.reshape(R, ncp).astype(BF16), vc_ref[0, g], preferred_element_type=F32))

        imp = _dot_nt(overlap_t, jnp.sum(p, axis=0).astype(BF16))
        imp = jnp.where(eligible_t, imp + force_t, -jnp.inf)
        sel = jnp.zeros((nb, QB), F32)
        for _ in range(n_sel):
            mx = jnp.max(imp, axis=0, keepdims=True)
            first = jnp.min(jnp.where(imp == mx, jf, float(nb)), axis=0, keepdims=True)
            pick = (jf == first) & (mx > -jnp.inf)
            sel = jnp.where(pick, 1.0, sel)
            imp = jnp.where(pick, -jnp.inf, imp)
        sels.append(sel.T.astype(BF16))

        p, l = _softmax_rows(_dot_nt(qb, kw_ref[0, g, pl.ds(w0, wk), :]).reshape(Hg, QB, wk) + bias_w[None])
        o_w = jnp.dot(p.reshape(R, wk).astype(BF16), vw_ref[0, g, pl.ds(w0, wk), :], preferred_element_type=F32)
        o_ws.append(o_w / l.reshape(R, 1))

    jb = lax.broadcasted_iota(jnp.int32, (nb, tk), 0)
    kl = lax.broadcasted_iota(jnp.int32, (nb, tk), 1)
    kl_row = lax.broadcasted_iota(jnp.int32, (1, tk), 1)

    def sel_step(kt, carry):
        k0 = pl.multiple_of(kt * tk, tk)
        expand = jnp.where(jb == jnp.right_shift(k0 + kl, SEL_SHIFT), 1.0, 0.0).astype(BF16)
        causal = k0 + kl_row <= t_q
        out = []
        for g in range(G):
            m, l, acc = carry[g]
            chosen = jnp.dot(sels[g], expand, preferred_element_type=F32)
            bias = jnp.where((chosen > 0.5) & causal, 0.0, NEG)
            s = _dot_nt(qbs[g], ks_ref[0, g, pl.ds(k0, tk), :]).reshape(Hg, QB, tk) + bias[None]
            p, m, l, alpha = _online_softmax(s, m, l)
            pv = jnp.dot(p.reshape(R, tk).astype(BF16), vs_ref[0, g, pl.ds(k0, tk), :], preferred_element_type=F32)
            out.append((m, l, alpha * acc + pv.reshape(Hg, QB, Dh)))
        return tuple(out)

    init = tuple((jnp.full((Hg, QB, 1), NEG, F32), jnp.zeros((Hg, QB, 1), F32), jnp.zeros((Hg, QB, Dh), F32))
                 for _ in range(G))
    final = lax.fori_loop(0, (s0 + QB + tk - 1) // tk, sel_step, init)

    gates = jax.nn.sigmoid(gl_ref[...])
    for g in range(G):
        o_s = (final[g][2] / final[g][1]).reshape(R, Dh)
        for h in range(Hg):
            c0 = (g * Hg + h) * N_NSA_BRANCHES
            rs = slice(h * QB, (h + 1) * QB)
            o_ref[:, (g * Hg + h) * Dh:(g * Hg + h + 1) * Dh] = (
                gates[:, c0:c0 + 1] * o_cs[g][rs] + gates[:, c0 + 1:c0 + 2] * o_s[rs]
                + gates[:, c0 + 2:c0 + 3] * o_ws[g][rs])


def nsa_attention(q, proj, kc, vc, ks, vs, kw, vw, B, S):
    G, Hg, Dh, QB = NSA_KV_GROUPS, NSA_HEADS_PER_GROUP, NSA_HEAD_DIM, Q_BLOCK
    nq = S // QB
    ncp = kc.shape[2]
    n_sel = min(N_SEL_BLOCKS, S // SEL_BLOCK)
    qcols = G * Hg * Dh
    kv_spec = pl.BlockSpec((1, G, S, Dh), lambda b, i: (b, 0, 0, 0), pipeline_mode=pl.Buffered(1))
    c_spec = pl.BlockSpec((1, G, ncp, Dh), lambda b, i: (b, 0, 0, 0), pipeline_mode=pl.Buffered(1))
    sw = max(TK_ATT, ncp, WINDOW + QB)
    vmem = G * (4 * S * Dh * 2 + 2 * ncp * Dh * 2) + 10 * G * Hg * QB * sw * 4
    return pl.pallas_call(
        functools.partial(_nsa_attn_kernel, seq=S, n_sel=n_sel),
        grid=(B, nq),
        in_specs=[pl.BlockSpec((1, G, Hg, QB, Dh), lambda b, i: (b, 0, 0, i, 0)),
                  pl.BlockSpec((QB, LANES), lambda b, i: (b * nq + i, P_G0 // LANES)),
                  c_spec, c_spec, kv_spec, kv_spec, kv_spec, kv_spec],
        out_specs=pl.BlockSpec((QB, qcols), lambda b, i: (b * nq + i, 0)),
        out_shape=jax.ShapeDtypeStruct((B * S, qcols), F32),
        compiler_params=_cparams(("parallel", "arbitrary"), vmem),
        name="nsa_attention",
    )(q, proj, kc, vc, ks, vs, kw, vw)


def _mla_q_kernel(qa_ref, ga_ref, w_ref, gq_ref, rope_ref, o_ref):
    W = MLA_QK_PAD
    c = _rms(qa_ref[...], ga_ref[...]).astype(BF16)
    q = jnp.dot(c, w_ref[...], preferred_element_type=F32)
    tab = rope_ref[...]
    fold = MLA_QK_DIM ** -0.5 * LOG2E
    for h in range(MLA_HEADS):
        qh = q[:, h * W:(h + 1) * W]
        ms = jnp.sum(qh * qh, axis=-1, keepdims=True) * (1.0 / MLA_QK_DIM)
        qn = qh * lax.rsqrt(ms + NORM_EPS) * gq_ref[...]
        o_ref[0, h, :, :LANES] = (qn[:, :LANES] * fold).astype(BF16)
        o_ref[0, h, :, LANES:] = (_rope_lanes(qn[:, LANES:], tab, MLA_ROPE_DIM // 2) * fold).astype(BF16)


def mla_q_prep(proj, ga, wq, gq, rope_m, B, S):
    tm = min(TM, S)
    ns = S // tm
    H, R, W = MLA_HEADS, MLA_Q_RANK, MLA_QK_PAD
    return pl.pallas_call(
        _mla_q_kernel,
        grid=(B * ns,),
        in_specs=[pl.BlockSpec((tm, R), lambda r: (r, P_QA0 // R)),
                  pl.BlockSpec((1, R), lambda r: (0, 0)),
                  pl.BlockSpec((R, H * W), lambda r: (0, 0)),
                  pl.BlockSpec((1, W), lambda r: (0, 0)),
                  pl.BlockSpec((3, tm, LANES), lambda r: (0, r, 0))],
        out_specs=pl.BlockSpec((1, H, tm, W), lambda r: (r // ns, 0, r % ns, 0)),
        out_shape=jax.ShapeDtypeStruct((B, H, S, W), BF16),
        compiler_params=_cparams(("parallel",), 2 * (tm * R * 4 + R * H * W * 2 + tm * H * W * 2) + 3 * tm * H * W * 4),
        name="mla_q_prep",
    )(proj, ga, wq, gq, rope_m)


def _mla_kv_kernel(kva_ref, kr_ref, ga_ref, w_ref, gk_ref, rope_ref, k_ref, v_ref):
    W = MLA_QK_PAD
    c = _rms(kva_ref[...], ga_ref[...]).astype(BF16)
    kv = jnp.dot(c, w_ref[...], preferred_element_type=F32)
    tab = rope_ref[...]
    k_rot = kr_ref[...]
    ss_rot = jnp.sum(k_rot * k_rot, axis=-1, keepdims=True)
    for h in range(MLA_HEADS):
        k_nope = kv[:, h * W:h * W + LANES]
        ms = (jnp.sum(k_nope * k_nope, axis=-1, keepdims=True) + ss_rot) * (1.0 / MLA_QK_DIM)
        r = lax.rsqrt(ms + NORM_EPS)
        k_ref[0, h, :, :LANES] = (k_nope * r * gk_ref[:, :LANES]).astype(BF16)
        k_ref[0, h, :, LANES:] = _rope_lanes(k_rot * r * gk_ref[:, LANES:], tab, MLA_ROPE_DIM // 2).astype(BF16)
        v_ref[0, h] = kv[:, h * W + LANES:(h + 1) * W].astype(BF16)


def mla_kv_prep(proj, ga, wkv, gk, rope_m, B, S):
    tm = min(TM, S)
    ns = S // tm
    H, R, W = MLA_HEADS, MLA_KV_RANK, MLA_QK_PAD
    return pl.pallas_call(
        _mla_kv_kernel,
        grid=(B * ns,),
        in_specs=[pl.BlockSpec((tm, R), lambda r: (r, P_KVA0 // R)),
                  pl.BlockSpec((tm, LANES), lambda r: (r, P_KR0 // LANES)),
                  pl.BlockSpec((1, R), lambda r: (0, 0)),
                  pl.BlockSpec((R, H * W), lambda r: (0, 0)),
                  pl.BlockSpec((1, W), lambda r: (0, 0)),
                  pl.BlockSpec((3, tm, LANES), lambda r: (0, r, 0))],
        out_specs=[pl.BlockSpec((1, H, tm, W), lambda r: (r // ns, 0, r % ns, 0)),
                   pl.BlockSpec((1, H, tm, MLA_V_DIM), lambda r: (r // ns, 0, r % ns, 0))],
        out_shape=[jax.ShapeDtypeStruct((B, H, S, W), BF16), jax.ShapeDtypeStruct((B, H, S, MLA_V_DIM), BF16)],
        compiler_params=_cparams(("parallel",), 2 * (tm * R * 4 + R * H * W * 2 + tm * H * W * 3) + 3 * tm * H * W * 4),
        name="mla_kv_prep",
    )(proj, proj, ga, wkv, gk, rope_m)


def _mla_attn_kernel(q_ref, k_ref, v_ref, o_ref, *, tq, heads):
    i = pl.program_id(2)
    Dv = v_ref.shape[-1]

    def tile(k0, carry, bias):
        out = []
        for c in range(heads):
            m, l, acc = carry[c]
            s = _dot_nt(q_ref[0, c], k_ref[0, c, pl.ds(k0, tq), :])
            if bias is not None:
                s = s + bias
            p, m, l, alpha = _online_softmax(s, m, l)
            pv = jnp.dot(p.astype(BF16), v_ref[0, c, pl.ds(k0, tq), :], preferred_element_type=F32)
            out.append((m, l, alpha * acc + pv))
        return tuple(out)

    init = tuple((jnp.full((tq, 1), NEG, F32), jnp.zeros((tq, 1), F32), jnp.zeros((tq, Dv), F32))
                 for _ in range(heads))
    carry = lax.fori_loop(0, i, lambda kt, c: tile(pl.multiple_of(kt * tq, tq), c, None), init)
    causal = jnp.where(lax.broadcasted_iota(jnp.int32, (tq, tq), 1) <= lax.broadcasted_iota(jnp.int32, (tq, tq), 0),
                       0.0, NEG)
    carry = tile(pl.multiple_of(i * tq, tq), carry, causal)
    for c in range(heads):
        o_ref[:, c * Dv:(c + 1) * Dv] = carry[c][2] / carry[c][1]


def mla_attention(q, k, v):
    B, H, S, W = q.shape
    tq = min(TQ_MLA, S)
    nq = S // tq
    Dv = v.shape[-1]
    hp = MLA_HEADS_PER_STEP
    vmem = hp * (2 * (S * W * 2 + S * Dv * 2) + 4 * tq * W * 2 + 8 * tq * tq * 4)
    return pl.pallas_call(
        functools.partial(_mla_attn_kernel, tq=tq, heads=hp),
        grid=(B, H // hp, nq),
        in_specs=[pl.BlockSpec((1, hp, tq, W), lambda b, h, i: (b, h, i, 0)),
                  pl.BlockSpec((1, hp, S, W), lambda b, h, i: (b, h, 0, 0), pipeline_mode=pl.Buffered(1)),
                  pl.BlockSpec((1, hp, S, Dv), lambda b, h, i: (b, h, 0, 0), pipeline_mode=pl.Buffered(1))],
        out_specs=pl.BlockSpec((tq, hp * Dv), lambda b, h, i: (b * nq + i, h)),
        out_shape=jax.ShapeDtypeStruct((B * S, H * Dv), F32),
        compiler_params=_cparams(("parallel", "parallel", "arbitrary"), vmem),
        name="mla_attention",
    )(q, k, v)


def _out_proj_kernel(x_ref, a_ref, b_ref, ga_ref, gb_ref, wa_ref, wb_ref, o_ref, na_ref, nb_ref):
    @pl.when(pl.program_id(1) == 0)
    def _():
        na_ref[...] = _rms(a_ref[...], ga_ref[...]).astype(BF16)
        nb_ref[...] = _rms(b_ref[...], gb_ref[...]).astype(BF16)

    o_ref[...] = (x_ref[...] + jnp.dot(na_ref[...], wa_ref[...], preferred_element_type=F32)
                  + jnp.dot(nb_ref[...], wb_ref[...], preferred_element_type=F32))


def out_proj(x, o_nsa, o_mla, g_nsa, g_mla, w_out):
    T, D = x.shape
    Ka, Kb = o_nsa.shape[1], o_mla.shape[1]
    assert Ka == Kb
    tm, tn = min(TM, T), TN_OUT
    vmem = 2 * (tm * tn * 8 + 2 * tm * Ka * 4 + 2 * Ka * tn * 2) + 2 * tm * Ka * 2
    return pl.pallas_call(
        _out_proj_kernel,
        grid=(T // tm, D // tn),
        in_specs=[pl.BlockSpec((tm, tn), lambda i, j: (i, j)),
                  pl.BlockSpec((tm, Ka), lambda i, j: (i, 0)),
                  pl.BlockSpec((tm, Kb), lambda i, j: (i, 0)),
                  pl.BlockSpec((1, Ka), lambda i, j: (0, 0)),
                  pl.BlockSpec((1, Kb), lambda i, j: (0, 0)),
                  pl.BlockSpec((Ka, tn), lambda i, j: (0, j)),
                  pl.BlockSpec((Kb, tn), lambda i, j: (1, j))],
        out_specs=pl.BlockSpec((tm, tn), lambda i, j: (i, j)),
        out_shape=jax.ShapeDtypeStruct((T, D), F32),
        scratch_shapes=[pltpu.VMEM((tm, Ka), BF16), pltpu.VMEM((tm, Kb), BF16)],
        compiler_params=_cparams(("parallel", "arbitrary"), 2 * vmem),
        name="out_proj",
    )(x, o_nsa, o_mla, g_nsa, g_mla, w_out, w_out)


def _swiglu_kernel(be_ref, x_ref, g_ref, wg_ref, wu_ref, wd_ref, o_ref, h_ref, acc_ref, *, residual):
    i = pl.program_id(0)
    f = pl.program_id(1)
    live = i < be_ref[0]

    @pl.when(live & (f == 0))
    def _():
        h_ref[...] = _rms(x_ref[...], g_ref[...]).astype(BF16)
        acc_ref[...] = jnp.zeros_like(acc_ref)

    @pl.when(live)
    def _():
        h = h_ref[...]
        a = jax.nn.silu(jnp.dot(h, wg_ref[0], preferred_element_type=F32)) * jnp.dot(
            h, wu_ref[0], preferred_element_type=F32)
        acc_ref[...] += jnp.dot(a.astype(BF16), wd_ref[0], preferred_element_type=F32)

    last = f == pl.num_programs(1) - 1

    @pl.when(live & last)
    def _():
        o_ref[...] = x_ref[...] + acc_ref[...] if residual else acc_ref[...]

    @pl.when(jnp.logical_not(live) & last)
    def _():
        o_ref[...] = jnp.zeros_like(o_ref)


def swiglu_blocks(x, g, wg, wu, wd, block_info, tm, residual):
    R, D = x.shape
    F = wg.shape[2]
    tf = TF
    nf = F // tf

    def wcol(i, f, be):
        live = i < be[0]
        return (be[1 + i], 0, jnp.where(live, f, nf - 1))

    def wrow(i, f, be):
        live = i < be[0]
        return (be[1 + i], jnp.where(live, f, nf - 1), 0)

    vmem = 4 * tm * D * 4 + tm * D * 2 + tm * D * 4 + 2 * 3 * D * tf * 2 + 6 * tm * tf * 4
    grid_spec = pltpu.PrefetchScalarGridSpec(
        num_scalar_prefetch=1,
        grid=(R // tm, nf),
        in_specs=[pl.BlockSpec((tm, D), lambda i, f, be: (i, 0)),
                  pl.BlockSpec((1, D), lambda i, f, be: (0, 0)),
                  pl.BlockSpec((1, D, tf), wcol),
                  pl.BlockSpec((1, D, tf), wcol),
                  pl.BlockSpec((1, tf, D), wrow)],
        out_specs=pl.BlockSpec((tm, D), lambda i, f, be: (i, 0)),
        scratch_shapes=[pltpu.VMEM((tm, D), BF16), pltpu.VMEM((tm, D), F32)],
    )
    return pl.pallas_call(
        functools.partial(_swiglu_kernel, residual=residual),
        grid_spec=grid_spec,
        out_shape=jax.ShapeDtypeStruct((R, D), F32),
        compiler_params=_cparams(("parallel", "arbitrary"), vmem + (4 << 20)),
        name="swiglu",
    )(block_info, x, g, wg, wu, wd)


def _router_kernel(x_ref, g_ref, w_ref, o_ref):
    h = _rms(x_ref[...], g_ref[...]).astype(BF16)
    logits = jnp.dot(h, w_ref[...], preferred_element_type=F32)
    lane = lax.broadcasted_iota(jnp.int32, logits.shape, 1)
    lf = lane.astype(F32)
    lg = jnp.where(lane < N_EXPERTS, logits, -jnp.inf)
    m1 = jnp.max(lg, axis=-1, keepdims=True)
    i1 = jnp.min(jnp.where(lg == m1, lf, float(LANES)), axis=-1, keepdims=True)
    lg2 = jnp.where(lf == i1, -jnp.inf, lg)
    m2 = jnp.max(lg2, axis=-1, keepdims=True)
    i2 = jnp.min(jnp.where(lg2 == m2, lf, float(LANES)), axis=-1, keepdims=True)
    e2 = jnp.exp(m2 - m1)
    den = 1.0 + e2
    o_ref[...] = jnp.where(lane == 0, i1, jnp.where(lane == 1, i2, jnp.where(lane == 2, 1.0 / den, e2 / den)))


def moe_router(x, g, w_router_p):
    T, D = x.shape
    tm = min(TM, T)
    return pl.pallas_call(
        _router_kernel,
        grid=(T // tm,),
        in_specs=[pl.BlockSpec((tm, D), lambda i: (i, 0)),
                  pl.BlockSpec((1, D), lambda i: (0, 0)),
                  pl.BlockSpec((D, LANES), lambda i: (0, 0))],
        out_specs=pl.BlockSpec((tm, LANES), lambda i: (i, 0)),
        out_shape=jax.ShapeDtypeStruct((T, LANES), F32),
        compiler_params=_cparams(("parallel",), 4 * tm * D * 4),
        name="moe_router",
    )(x, g, w_router_p)


def _gather_kernel(idx_ref, src_ref, dst_ref, sem):
    base = pl.program_id(0) * GATHER_ROWS

    def row_copy(r):
        return pltpu.make_async_copy(src_ref.at[idx_ref[base + r]], dst_ref.at[r], sem)

    def start(r, c):
        row_copy(r).start()
        return c

    def wait(r, c):
        row_copy(r).wait()
        return c

    lax.fori_loop(0, GATHER_ROWS, start, 0, unroll=GATHER_UNROLL)
    lax.fori_loop(0, GATHER_ROWS, wait, 0, unroll=GATHER_UNROLL)


def gather_rows(src, idx):
    n = idx.shape[0]
    D = src.shape[1]
    src = src.reshape(src.shape[0], D // LANES, LANES)
    grid_spec = pltpu.PrefetchScalarGridSpec(
        num_scalar_prefetch=1,
        grid=(n // GATHER_ROWS,),
        in_specs=[pl.BlockSpec(memory_space=pl.ANY)],
        out_specs=pl.BlockSpec((GATHER_ROWS, D // LANES, LANES), lambda i, idx: (i, 0, 0)),
        scratch_shapes=[pltpu.SemaphoreType.DMA(())],
    )
    return pl.pallas_call(
        _gather_kernel,
        grid_spec=grid_spec,
        out_shape=jax.ShapeDtypeStruct((n, D // LANES, LANES), src.dtype),
        compiler_params=_cparams(("arbitrary",), 4 * GATHER_ROWS * D * 4),
        name="gather_rows",
    )(idx, src).reshape(n, D)


def _combine_kernel(x_ref, ya_ref, yb_ref, gate_ref, o_ref):
    gate = gate_ref[...]
    o_ref[...] = x_ref[...] + (ya_ref[...] * gate[:, 2:3] + yb_ref[...] * gate[:, 3:4])


def moe_combine(x, y_pair, route):
    T, D = x.shape
    tm = min(TM, T)
    nt = T // tm
    return pl.pallas_call(
        _combine_kernel,
        grid=(nt,),
        in_specs=[pl.BlockSpec((tm, D), lambda i: (i, 0)),
                  pl.BlockSpec((tm, D), lambda i: (i, 0)),
                  pl.BlockSpec((tm, D), lambda i: (i + nt, 0)),
                  pl.BlockSpec((tm, LANES), lambda i: (i, 0))],
        out_specs=pl.BlockSpec((tm, D), lambda i: (i, 0)),
        out_shape=jax.ShapeDtypeStruct((T, D), F32),
        compiler_params=_cparams(("parallel",), 10 * tm * D * 4),
        name="moe_combine",
    )(x, y_pair, y_pair, route)


def moe_ffn(x, g, w_router_p, wg, wu, wd, expert_base):
    T, D = x.shape
    n_assign = T * TOP_K
    route = moe_router(x, g, w_router_p)
    flat_e = route[:, :TOP_K].astype(jnp.int32).reshape(-1)
    order = jnp.argsort(flat_e)
    se = flat_e[order]
    stok = (order // TOP_K).astype(jnp.int32)
    counts = jnp.bincount(flat_e, length=N_EXPERTS)
    padded = (counts + MOE_TM - 1) // MOE_TM * MOE_TM
    pad_end = jnp.cumsum(padded)
    pad_start = pad_end - padded
    start = jnp.cumsum(counts) - counts
    dest = (pad_start[se] + jnp.arange(n_assign) - start[se]).astype(jnp.int32)
    n_blocks = -(-(n_assign + N_EXPERTS * (MOE_TM - 1)) // MOE_TM)
    n_rows = n_blocks * MOE_TM
    block_e = jnp.minimum(jnp.searchsorted(pad_end, jnp.arange(n_blocks) * MOE_TM, side='right'), N_EXPERTS - 1)
    n_live = (pad_end[-1] // MOE_TM).astype(jnp.int32)
    block_info = jnp.concatenate([n_live[None], block_e.astype(jnp.int32) + expert_base])
    row_e = jnp.repeat(block_e, MOE_TM)
    row_j = jnp.arange(n_rows) - pad_start[row_e]
    row_src = jnp.clip(start[row_e] + row_j, 0, n_assign - 1)
    row_tok = jnp.where(row_j < counts[row_e], stok[row_src], 0).astype(jnp.int32)
    dest_flat = dest[jnp.argsort(order)]
    pair_idx = jnp.concatenate([dest_flat[0::2], dest_flat[1::2]])

    x_rows = gather_rows(x, row_tok)
    y_rows = swiglu_blocks(x_rows, g, wg, wu, wd, block_info, MOE_TM, residual=False)
    y_pair = gather_rows(y_rows, pair_idx)
    return moe_combine(x, y_pair, route)


def _rope_tab(pos, rot_dim):
    half = rot_dim // 2
    inv_freq = jnp.power(jnp.float32(ROPE_THETA), -jnp.arange(0, rot_dim, 2, dtype=F32) / rot_dim)
    ang = pos.astype(F32)[..., None] * inv_freq
    cos, sin = jnp.cos(ang), jnp.sin(ang)
    rest = LANES - rot_dim
    one = jnp.ones(pos.shape + (rest,), F32)
    zr = jnp.zeros(pos.shape + (rest,), F32)
    zh = jnp.zeros_like(sin)
    return jnp.stack([jnp.concatenate([cos, cos, one], -1),
                      jnp.concatenate([zh, sin, zr], -1),
                      jnp.concatenate([-sin, zh, zr], -1)])


def _pad_cols(w, n):
    return jnp.pad(w, ((0, 0), (0, n - w.shape[1])))


def kernel(x, positions, attn_norm, w_in, nsa_q_norm, nsa_k_norm, cmp_k_w1, cmp_k_w2, cmp_k_pos, cmp_v_w1, cmp_v_w2, cmp_v_pos, mla_q_a_norm, mla_w_q_up, mla_kv_a_norm, mla_w_kv_up, mla_q_norm, mla_k_norm, nsa_out_norm, mla_out_norm, w_out, ffn_norm, dense_w_gate, dense_w_up, dense_w_down, moe_router, moe_w_gate, moe_w_up, moe_w_down):
    B, S, D = x.shape
    T = B * S
    depth = w_in.shape[0]
    Dh = NSA_HEAD_DIM
    ncp = S // CMP_STRIDE
    kw = CMP_STRIDE * Dh

    rope_n = _rope_tab(positions, NSA_ROT_DIM).reshape(3, T, LANES)
    rope_m = _rope_tab(positions, MLA_ROPE_DIM).reshape(3, T, LANES)
    cmp_idx = jnp.minimum(jnp.arange(ncp) * CMP_STRIDE + CMP_BLOCK - 1, S - 1)
    rope_c = _rope_tab(positions[:, cmp_idx], NSA_ROT_DIM)

    dense_g, dense_u, dense_d = (w.astype(BF16) for w in (dense_w_gate, dense_w_up, dense_w_down))
    moe_g, moe_u, moe_d = (w.reshape((-1,) + w.shape[2:]).astype(BF16) for w in (moe_w_gate, moe_w_up, moe_w_down))

    g0 = NSA_Q_COLS + NSA_KV_COLS
    m0 = g0 + NSA_GATE_COLS
    xt = x.reshape(T, D)
    for layer in range(depth):
        w = w_in[layer]
        w_in_p = jnp.concatenate([w[:, :g0], w[:, m0:m0 + MLA_Q_RANK], w[:, m0 + MLA_Q_RANK:m0 + MLA_Q_RANK + MLA_KV_RANK],
                                  _pad_cols(w[:, m0 + MLA_Q_RANK + MLA_KV_RANK:], LANES),
                                  _pad_cols(w[:, g0:m0], LANES)], axis=1).astype(BF16)
        proj = rms_matmul(xt, attn_norm[layer][None], w_in_p, min(TM, T), TN_PROJ)

        gk = nsa_k_norm[layer]
        q_n, kc_raw, vc_raw, ks, vs, kwin, vwin = nsa_prep(proj, rope_n, nsa_q_norm[layer][None], gk, B, S)
        kc, vc = nsa_compress(
            kc_raw, vc_raw,
            cmp_k_w1[layer].reshape(2, kw, Dh).astype(BF16), cmp_k_w2[layer].astype(BF16),
            cmp_k_pos[layer].reshape(2, 1, kw),
            cmp_v_w1[layer].reshape(2, kw, Dh).astype(BF16), cmp_v_w2[layer].astype(BF16),
            cmp_v_pos[layer].reshape(2, 1, kw), rope_c, gk)
        o_nsa = nsa_attention(q_n, proj, kc, vc, ks, vs, kwin, vwin, B, S)

        wq = jnp.pad(mla_w_q_up[layer].reshape(MLA_Q_RANK, MLA_HEADS, MLA_QK_DIM),
                     ((0, 0), (0, 0), (0, MLA_QK_PAD - MLA_QK_DIM))).reshape(MLA_Q_RANK, MLA_HEADS * MLA_QK_PAD)
        gq = jnp.pad(mla_q_norm[layer], (0, MLA_QK_PAD - MLA_QK_DIM))[None]
        gkm = jnp.pad(mla_k_norm[layer], (0, MLA_QK_PAD - MLA_QK_DIM))[None]
        q_m = mla_q_prep(proj, mla_q_a_norm[layer][None], wq.astype(BF16), gq, rope_m, B, S)
        k_m, v_m = mla_kv_prep(proj, mla_kv_a_norm[layer][None], mla_w_kv_up[layer].astype(BF16), gkm, rope_m, B, S)
        o_mla = mla_attention(q_m, k_m, v_m)

        xt = out_proj(xt, o_nsa, o_mla, nsa_out_norm[layer][None], mla_out_norm[layer][None],
                      w_out[layer].astype(BF16))

        i = layer // 2
        gf = ffn_norm[layer][None]
        if layer % 2 == 0:
            tm = min(TM, T)
            info = jnp.concatenate([jnp.full((1,), T // tm, jnp.int32), jnp.full((T // tm,), i, jnp.int32)])
            xt = swiglu_blocks(xt, gf, dense_g, dense_u, dense_d, info, tm, residual=True)
        else:
            xt = moe_ffn(xt, gf, _pad_cols(moe_router[i], LANES).astype(BF16), moe_g, moe_u, moe_d, i * N_EXPERTS)
    return xt.reshape(B, S, D)
```

```python
import functools

import jax
import jax.numpy as jnp
from jax import lax
from jax.experimental import pallas as pl
from jax.experimental.pallas import tpu as pltpu

F32 = jnp.float32
BF16 = jnp.bfloat16

LANES = 128
VMEM_CAP_BYTES = 56 * 1024 * 1024

LOG2E = 1.4426950408889634
NEG = -1e30

NORM_EPS = 1e-6
ROPE_THETA = 500000.0
Q_BLOCK = 128

NSA_HEAD_DIM = 128
NSA_KV_GROUPS = 2
NSA_HEADS_PER_GROUP = 4
NSA_HEADS = NSA_KV_GROUPS * NSA_HEADS_PER_GROUP
NSA_ROT_DIM = NSA_HEAD_DIM // 4
N_NSA_BRANCHES = 3
CMP_BLOCK = 32
CMP_STRIDE = 16
SEL_BLOCK = 64
SEL_SHIFT = SEL_BLOCK.bit_length() - 1
assert 1 << SEL_SHIFT == SEL_BLOCK
N_SEL_BLOCKS = 16
WINDOW = 512
SEL_FORCE = 1.0e4

MLA_V_DIM = 128
MLA_HEADS = 8
MLA_Q_RANK = 512
MLA_KV_RANK = 512
MLA_NOPE_DIM = 128
MLA_ROPE_DIM = 64
MLA_QK_DIM = MLA_NOPE_DIM + MLA_ROPE_DIM
MLA_QK_PAD = 256

N_EXPERTS = 8
TOP_K = 2

NSA_Q_COLS = NSA_HEADS * NSA_HEAD_DIM
NSA_KV_COLS = N_NSA_BRANCHES * 2 * NSA_KV_GROUPS * NSA_HEAD_DIM
NSA_GATE_COLS = NSA_HEADS * N_NSA_BRANCHES
P_Q0 = 0
P_KV0 = P_Q0 + NSA_Q_COLS
P_QA0 = P_KV0 + NSA_KV_COLS
P_KVA0 = P_QA0 + MLA_Q_RANK
P_KR0 = P_KVA0 + MLA_KV_RANK
P_G0 = P_KR0 + LANES
P_COLS = P_G0 + LANES

TM = 512
TN_PROJ = 768
TN_OUT = 512
TF = 512
TS_PREP = 1024
TK_ATT = 512
TQ_MLA = 512
MLA_HEADS_PER_STEP = 4
MOE_TM = 512
COMBINE_TM = 256
GATHER_UNROLL = 8


def _cparams(sem, vmem_bytes):
    return pltpu.CompilerParams(dimension_semantics=sem,
                                vmem_limit_bytes=int(min(VMEM_CAP_BYTES, max(vmem_bytes, 16 * 1024 * 1024))))


def _rms(x, g):
    ms = jnp.mean(x * x, axis=-1, keepdims=True)
    return x * lax.rsqrt(ms + NORM_EPS) * g


def _rope_lanes(x, tab, half):
    return x * tab[0] + pltpu.roll(x, half, 1) * tab[1] + pltpu.roll(x, LANES - half, 1) * tab[2]


def _dot_nt(a, b):
    return lax.dot_general(a, b, (((1,), (1,)), ((), ())), preferred_element_type=F32)


def _rms_matmul_kernel(x_ref, g_ref, w_ref, o_ref, h_ref):
    @pl.when(pl.program_id(1) == 0)
    def _():
        h_ref[...] = _rms(x_ref[...], g_ref[...]).astype(BF16)

    o_ref[...] = jnp.dot(h_ref[...], w_ref[...], preferred_element_type=F32)


def rms_matmul(x, g, w, tm, tn):
    T, K = x.shape
    N = w.shape[1]
    vmem = 2 * tm * K * 4 + tm * K * 2 + 2 * K * tn * 2 + 2 * tm * tn * 4
    return pl.pallas_call(
        _rms_matmul_kernel,
        grid=(T // tm, N // tn),
        in_specs=[pl.BlockSpec((tm, K), lambda i, j: (i, 0)),
                  pl.BlockSpec((1, K), lambda i, j: (0, 0)),
                  pl.BlockSpec((K, tn), lambda i, j: (0, j))],
        out_specs=pl.BlockSpec((tm, tn), lambda i, j: (i, j)),
        out_shape=jax.ShapeDtypeStruct((T, N), F32),
        scratch_shapes=[pltpu.VMEM((tm, K), BF16)],
        compiler_params=_cparams(("parallel", "arbitrary"), 2 * vmem),
        name="rms_matmul",
    )(x, g, w)


def _nsa_prep_kernel(q_in, kc_in, vc_in, ks_in, vs_in, kw_in, vw_in, rope_ref, gq_ref, gk_ref,
                     q_out, kc_out, vc_out, ks_out, vs_out, kw_out, vw_out):
    tab = rope_ref[...]
    half = NSA_ROT_DIM // 2
    Dh = NSA_HEAD_DIM
    fold = Dh ** -0.5 * LOG2E
    for h in range(NSA_HEADS_PER_GROUP):
        qh = _rope_lanes(_rms(q_in[:, h * Dh:(h + 1) * Dh], gq_ref[...]), tab, half)
        q_out[0, 0, h] = (qh * fold).astype(BF16)
    kc_out[0, 0] = kc_in[...]
    vc_out[0, 0] = vc_in[...]
    ks_out[0, 0] = _rope_lanes(_rms(ks_in[...], gk_ref[1:2, :]), tab, half).astype(BF16)
    kw_out[0, 0] = _rope_lanes(_rms(kw_in[...], gk_ref[2:3, :]), tab, half).astype(BF16)
    vs_out[0, 0] = vs_in[...].astype(BF16)
    vw_out[0, 0] = vw_in[...].astype(BF16)


def nsa_prep(proj, rope_n, gq, gk, B, S):
    ts = min(TS_PREP, S)
    ns = S // ts
    G, Hg, Dh = NSA_KV_GROUPS, NSA_HEADS_PER_GROUP, NSA_HEAD_DIM
    kv_blk0 = P_KV0 // Dh

    def in_spec(branch, kv):
        off = kv_blk0 + (branch * 2 + kv) * G
        return pl.BlockSpec((ts, Dh), lambda b, s, g: (b * ns + s, off + g))

    out_spec = pl.BlockSpec((1, 1, ts, Dh), lambda b, s, g: (b, g, s, 0))
    f32_out = jax.ShapeDtypeStruct((B, G, S, Dh), F32)
    bf_out = jax.ShapeDtypeStruct((B, G, S, Dh), BF16)
    return pl.pallas_call(
        _nsa_prep_kernel,
        grid=(B, ns, G),
        in_specs=[pl.BlockSpec((ts, Hg * Dh), lambda b, s, g: (b * ns + s, P_Q0 // (Hg * Dh) + g)),
                  in_spec(0, 0), in_spec(0, 1), in_spec(1, 0), in_spec(1, 1), in_spec(2, 0), in_spec(2, 1),
                  pl.BlockSpec((3, ts, Dh), lambda b, s, g: (0, b * ns + s, 0)),
                  pl.BlockSpec((1, Dh), lambda b, s, g: (0, 0)),
                  pl.BlockSpec((N_NSA_BRANCHES, Dh), lambda b, s, g: (0, 0))],
        out_specs=[pl.BlockSpec((1, 1, Hg, ts, Dh), lambda b, s, g: (b, g, 0, s, 0))] + [out_spec] * 6,
        out_shape=[jax.ShapeDtypeStruct((B, G, Hg, S, Dh), BF16), f32_out, f32_out, bf_out, bf_out, bf_out, bf_out],
        compiler_params=_cparams(("parallel", "parallel", "parallel"), 2 * 2 * 20 * ts * Dh * 4),
        name="nsa_prep",
    )(proj, proj, proj, proj, proj, proj, proj, rope_n, gq, gk)


def _compress_one(t_ref, w1_ref, w2_ref, pos_ref):
    t = t_ref[0, 0]
    ncp = t.shape[0]
    u = jnp.dot((t + pos_ref[0]).astype(BF16), w1_ref[0], preferred_element_type=F32)
    v = jnp.dot((t + pos_ref[1]).astype(BF16), w1_ref[1], preferred_element_type=F32)
    pre = u + pltpu.roll(v, ncp - 1, 0)
    return jnp.dot(jax.nn.gelu(pre).astype(BF16), w2_ref[...], preferred_element_type=F32)


def _compress_kernel(tk_ref, tv_ref, w1k_ref, w2k_ref, posk_ref, w1v_ref, w2v_ref, posv_ref, rope_ref, gk_ref,
                     kc_ref, vc_ref):
    k = _compress_one(tk_ref, w1k_ref, w2k_ref, posk_ref)
    kc_ref[0, 0] = _rope_lanes(_rms(k, gk_ref[0:1, :]), rope_ref[:, 0], NSA_ROT_DIM // 2).astype(BF16)
    vc_ref[0, 0] = _compress_one(tv_ref, w1v_ref, w2v_ref, posv_ref).astype(BF16)


def nsa_compress(kc_raw, vc_raw, w1k, w2k, posk, w1v, w2v, posv, rope_c, gk):
    B, G, S, Dh = kc_raw.shape
    ncp = S // CMP_STRIDE
    kw = CMP_STRIDE * Dh
    tk = kc_raw.reshape(B, G, ncp, kw)
    tv = vc_raw.reshape(B, G, ncp, kw)
    t_spec = pl.BlockSpec((1, 1, ncp, kw), lambda b, g: (b, g, 0, 0))
    w1_spec = pl.BlockSpec((2, kw, Dh), lambda b, g: (0, 0, 0))
    w2_spec = pl.BlockSpec((Dh, Dh), lambda b, g: (0, 0))
    pos_spec = pl.BlockSpec((2, 1, kw), lambda b, g: (0, 0, 0))
    out_spec = pl.BlockSpec((1, 1, ncp, Dh), lambda b, g: (b, g, 0, 0))
    out = jax.ShapeDtypeStruct((B, G, ncp, Dh), BF16)
    return pl.pallas_call(
        _compress_kernel,
        grid=(B, G),
        in_specs=[t_spec, t_spec, w1_spec, w2_spec, pos_spec, w1_spec, w2_spec, pos_spec,
                  pl.BlockSpec((3, 1, ncp, Dh), lambda b, g: (0, b, 0, 0)),
                  pl.BlockSpec((N_NSA_BRANCHES, Dh), lambda b, g: (0, 0))],
        out_specs=[out_spec, out_spec],
        out_shape=[out, out],
        compiler_params=_cparams(("parallel", "parallel"), 2 * (4 * ncp * kw * 4 + 8 * kw * Dh * 2)),
        name="nsa_compress",
    )(tk, tv, w1k, w2k, posk, w1v, w2v, posv, rope_c, gk)


def _online_softmax(s, m, l):
    m_new = jnp.maximum(m, jnp.max(s, axis=-1, keepdims=True))
    alpha = jnp.exp2(m - m_new)
    p = jnp.exp2(s - m_new)
    return p, m_new, alpha * l + jnp.sum(p, axis=-1, keepdims=True), alpha


def _softmax_rows(s):
    p = jnp.exp2(s - jnp.max(s, axis=-1, keepdims=True))
    return p, jnp.sum(p, axis=-1, keepdims=True)


def _nsa_attn_kernel(q_ref, gl_ref, kc_ref, vc_ref, ks_ref, vs_ref, kw_ref, vw_ref, o_ref,
                     *, seq, n_sel):
    G, Hg, Dh, QB = NSA_KV_GROUPS, NSA_HEADS_PER_GROUP, NSA_HEAD_DIM, Q_BLOCK
    R = Hg * QB
    s0 = pl.program_id(1) * QB
    ncp = kc_ref.shape[2]
    nb = max(LANES, seq // SEL_BLOCK)
    tk = TK_ATT
    wk = WINDOW + QB

    t_q = s0 + lax.broadcasted_iota(jnp.int32, (QB, 1), 0)
    t_l = s0 + lax.broadcasted_iota(jnp.int32, (1, QB), 1)

    cmp_end = lax.broadcasted_iota(jnp.int32, (1, ncp), 1) * CMP_STRIDE + (CMP_BLOCK - 1)
    bias_c = jnp.where(cmp_end <= t_q, 0.0, NEG)
    row_ok = jnp.where(t_q >= CMP_BLOCK - 1, 1.0, 0.0)
    w0 = pl.multiple_of(jnp.maximum(s0 - WINDOW, 0), QB)
    diff = t_q - (w0 + lax.broadcasted_iota(jnp.int32, (1, wk), 1))
    bias_w = jnp.where((diff >= 0) & (diff < WINDOW), 0.0, NEG)

    j_i = lax.broadcasted_iota(jnp.int32, (nb, ncp), 0) * SEL_BLOCK
    c_i = lax.broadcasted_iota(jnp.int32, (nb, ncp), 1) * CMP_STRIDE
    overlap_t = jnp.where((c_i < j_i + SEL_BLOCK) & (c_i + (CMP_BLOCK - 1) >= j_i), 1.0, 0.0).astype(BF16)
    jt = lax.broadcasted_iota(jnp.int32, (nb, QB), 0)
    cur = jnp.right_shift(t_l, SEL_SHIFT)
    force_t = jnp.where((jt == 0) | (jt == cur) | (jt == cur - 1), SEL_FORCE, 0.0)
    eligible_t = jt <= cur
    jf = jt.astype(F32)

    qbs, sels, o_cs, o_ws = [], [], [], []
    for g in range(G):
        qb = q_ref[0, g].reshape(R, Dh)
        qbs.append(qb)

        p, l = _softmax_rows(_dot_nt(qb, kc_ref[0, g]).reshape(Hg, QB, ncp) + bias_c[None])
        p = p * (row_ok[None] / l)
        o_cs.append(jnp.dot(p.reshape(R, ncp).astype(BF16), vc_ref[0, g], preferred_element_type=F32))

        imp = _dot_nt(overlap_t, jnp.sum(p, axis=0).astype(BF16))
        imp = jnp.where(eligible_t, imp + force_t, -jnp.inf)
        sel = jnp.zeros((nb, QB), F32)
        for _ in range(n_sel):
            mx = jnp.max(imp, axis=0, keepdims=True)
            first = jnp.min(jnp.where(imp == mx, jf, float(nb)), axis=0, keepdims=True)
            pick = (jf == first) & (mx > -jnp.inf)
            sel = jnp.where(pick, 1.0, sel)
            imp = jnp.where(pick, -jnp.inf, imp)
        sels.append(sel.T.astype(BF16))

        p, l = _softmax_rows(_dot_nt(qb, kw_ref[0, g, pl.ds(w0, wk), :]).reshape(Hg, QB, wk) + bias_w[None])
        o_w = jnp.dot(p.reshape(R, wk).astype(BF16), vw_ref[0, g, pl.ds(w0, wk), :], preferred_element_type=F32)
        o_ws.append(o_w / l.reshape(R, 1))

    jb = lax.broadcasted_iota(jnp.int32, (nb, tk), 0)
    kl = lax.broadcasted_iota(jnp.int32, (nb, tk), 1)
    kl_row = lax.broadcasted_iota(jnp.int32, (1, tk), 1)

    def sel_step(kt, carry):
        k0 = pl.multiple_of(kt * tk, tk)
        expand = jnp.where(jb == jnp.right_shift(k0 + kl, SEL_SHIFT), 1.0, 0.0).astype(BF16)
        causal = k0 + kl_row <= t_q
        out = []
        for g in range(G):
            m, l, acc = carry[g]
            chosen = jnp.dot(sels[g], expand, preferred_element_type=F32)
            bias = jnp.where((chosen > 0.5) & causal, 0.0, NEG)
            s = _dot_nt(qbs[g], ks_ref[0, g, pl.ds(k0, tk), :]).reshape(Hg, QB, tk) + bias[None]
            p, m, l, alpha = _online_softmax(s, m, l)
            pv = jnp.dot(p.reshape(R, tk).astype(BF16), vs_ref[0, g, pl.ds(k0, tk), :], preferred_element_type=F32)
            out.append((m, l, alpha * acc + pv.reshape(Hg, QB, Dh)))
        return tuple(out)

    init = tuple((jnp.full((Hg, QB, 1), NEG, F32), jnp.zeros((Hg, QB, 1), F32), jnp.zeros((Hg, QB, Dh), F32))
                 for _ in range(G))
    final = lax.fori_loop(0, (s0 + QB + tk - 1) // tk, sel_step, init)

    gates = jax.nn.sigmoid(gl_ref[...])
    for g in range(G):
        o_s = (final[g][2] / final[g][1]).reshape(R, Dh)
        for h in range(Hg):
            c0 = (g * Hg + h) * N_NSA_BRANCHES
            rs = slice(h * QB, (h + 1) * QB)
            o_ref[:, (g * Hg + h) * Dh:(g * Hg + h + 1) * Dh] = (
                gates[:, c0:c0 + 1] * o_cs[g][rs] + gates[:, c0 + 1:c0 + 2] * o_s[rs]
                + gates[:, c0 + 2:c0 + 3] * o_ws[g][rs])


def nsa_attention(q, proj, kc, vc, ks, vs, kw, vw, B, S):
    G, Hg, Dh, QB = NSA_KV_GROUPS, NSA_HEADS_PER_GROUP, NSA_HEAD_DIM, Q_BLOCK
    nq = S // QB
    ncp = kc.shape[2]
    n_sel = min(N_SEL_BLOCKS, S // SEL_BLOCK)
    qcols = G * Hg * Dh
    kv_spec = pl.BlockSpec((1, G, S, Dh), lambda b, i: (b, 0, 0, 0), pipeline_mode=pl.Buffered(1))
    c_spec = pl.BlockSpec((1, G, ncp, Dh), lambda b, i: (b, 0, 0, 0), pipeline_mode=pl.Buffered(1))
    sw = max(TK_ATT, ncp, WINDOW + QB)
    vmem = G * (4 * S * Dh * 2 + 2 * ncp * Dh * 2) + 10 * G * Hg * QB * sw * 4
    return pl.pallas_call(
        functools.partial(_nsa_attn_kernel, seq=S, n_sel=n_sel),
        grid=(B, nq),
        in_specs=[pl.BlockSpec((1, G, Hg, QB, Dh), lambda b, i: (b, 0, 0, i, 0)),
                  pl.BlockSpec((QB, LANES), lambda b, i: (b * nq + i, P_G0 // LANES)),
                  c_spec, c_spec, kv_spec, kv_spec, kv_spec, kv_spec],
        out_specs=pl.BlockSpec((QB, qcols), lambda b, i: (b * nq + i, 0)),
        out_shape=jax.ShapeDtypeStruct((B * S, qcols), F32),
        compiler_params=_cparams(("parallel", "arbitrary"), vmem),
        name="nsa_attention",
    )(q, proj, kc, vc, ks, vs, kw, vw)


def _mla_q_kernel(qa_ref, ga_ref, w_ref, gq_ref, rope_ref, o_ref):
    W = MLA_QK_PAD
    c = _rms(qa_ref[...], ga_ref[...]).astype(BF16)
    q = jnp.dot(c, w_ref[...], preferred_element_type=F32)
    tab = rope_ref[...]
    fold = MLA_QK_DIM ** -0.5 * LOG2E
    for h in range(MLA_HEADS):
        qh = q[:, h * W:(h + 1) * W]
        ms = jnp.sum(qh * qh, axis=-1, keepdims=True) * (1.0 / MLA_QK_DIM)
        qn = qh * lax.rsqrt(ms + NORM_EPS) * gq_ref[...]
        o_ref[0, h, :, :LANES] = (qn[:, :LANES] * fold).astype(BF16)
        o_ref[0, h, :, LANES:] = (_rope_lanes(qn[:, LANES:], tab, MLA_ROPE_DIM // 2) * fold).astype(BF16)


def mla_q_prep(proj, ga, wq, gq, rope_m, B, S):
    tm = min(TM, S)
    ns = S // tm
    H, R, W = MLA_HEADS, MLA_Q_RANK, MLA_QK_PAD
    return pl.pallas_call(
        _mla_q_kernel,
        grid=(B * ns,),
        in_specs=[pl.BlockSpec((tm, R), lambda r: (r, P_QA0 // R)),
                  pl.BlockSpec((1, R), lambda r: (0, 0)),
                  pl.BlockSpec((R, H * W), lambda r: (0, 0)),
                  pl.BlockSpec((1, W), lambda r: (0, 0)),
                  pl.BlockSpec((3, tm, LANES), lambda r: (0, r, 0))],
        out_specs=pl.BlockSpec((1, H, tm, W), lambda r: (r // ns, 0, r % ns, 0)),
        out_shape=jax.ShapeDtypeStruct((B, H, S, W), BF16),
        compiler_params=_cparams(("parallel",), 2 * (tm * R * 4 + R * H * W * 2 + tm * H * W * 2) + 3 * tm * H * W * 4),
        name="mla_q_prep",
    )(proj, ga, wq, gq, rope_m)


def _mla_kv_kernel(kva_ref, kr_ref, ga_ref, w_ref, gk_ref, rope_ref, k_ref, v_ref):
    W = MLA_QK_PAD
    c = _rms(kva_ref[...], ga_ref[...]).astype(BF16)
    kv = jnp.dot(c, w_ref[...], preferred_element_type=F32)
    tab = rope_ref[...]
    k_rot = kr_ref[...]
    ss_rot = jnp.sum(k_rot * k_rot, axis=-1, keepdims=True)
    for h in range(MLA_HEADS):
        k_nope = kv[:, h * W:h * W + LANES]
        ms = (jnp.sum(k_nope * k_nope, axis=-1, keepdims=True) + ss_rot) * (1.0 / MLA_QK_DIM)
        r = lax.rsqrt(ms + NORM_EPS)
        k_ref[0, h, :, :LANES] = (k_nope * r * gk_ref[:, :LANES]).astype(BF16)
        k_ref[0, h, :, LANES:] = _rope_lanes(k_rot * r * gk_ref[:, LANES:], tab, MLA_ROPE_DIM // 2).astype(BF16)
        v_ref[0, h] = kv[:, h * W + LANES:(h + 1) * W].astype(BF16)


def mla_kv_prep(proj, ga, wkv, gk, rope_m, B, S):
    tm = min(TM, S)
    ns = S // tm
    H, R, W = MLA_HEADS, MLA_KV_RANK, MLA_QK_PAD
    return pl.pallas_call(
        _mla_kv_kernel,
        grid=(B * ns,),
        in_specs=[pl.BlockSpec((tm, R), lambda r: (r, P_KVA0 // R)),
                  pl.BlockSpec((tm, LANES), lambda r: (r, P_KR0 // LANES)),
                  pl.BlockSpec((1, R), lambda r: (0, 0)),
                  pl.BlockSpec((R, H * W), lambda r: (0, 0)),
                  pl.BlockSpec((1, W), lambda r: (0, 0)),
                  pl.BlockSpec((3, tm, LANES), lambda r: (0, r, 0))],
        out_specs=[pl.BlockSpec((1, H, tm, W), lambda r: (r // ns, 0, r % ns, 0)),
                   pl.BlockSpec((1, H, tm, MLA_V_DIM), lambda r: (r // ns, 0, r % ns, 0))],
        out_shape=[jax.ShapeDtypeStruct((B, H, S, W), BF16), jax.ShapeDtypeStruct((B, H, S, MLA_V_DIM), BF16)],
        compiler_params=_cparams(("parallel",), 2 * (tm * R * 4 + R * H * W * 2 + tm * H * W * 3) + 3 * tm * H * W * 4),
        name="mla_kv_prep",
    )(proj, proj, ga, wkv, gk, rope_m)


def _mla_attn_kernel(q_ref, k_ref, v_ref, o_ref, *, tq, heads):
    i = pl.program_id(2)
    Dv = v_ref.shape[-1]

    def tile(k0, carry, bias):
        out = []
        for c in range(heads):
            m, l, acc = carry[c]
            s = _dot_nt(q_ref[0, c], k_ref[0, c, pl.ds(k0, tq), :])
            if bias is not None:
                s = s + bias
            p, m, l, alpha = _online_softmax(s, m, l)
            pv = jnp.dot(p.astype(BF16), v_ref[0, c, pl.ds(k0, tq), :], preferred_element_type=F32)
            out.append((m, l, alpha * acc + pv))
        return tuple(out)

    init = tuple((jnp.full((tq, 1), NEG, F32), jnp.zeros((tq, 1), F32), jnp.zeros((tq, Dv), F32))
                 for _ in range(heads))
    carry = lax.fori_loop(0, i, lambda kt, c: tile(pl.multiple_of(kt * tq, tq), c, None), init)
    causal = jnp.where(lax.broadcasted_iota(jnp.int32, (tq, tq), 1) <= lax.broadcasted_iota(jnp.int32, (tq, tq), 0),
                       0.0, NEG)
    carry = tile(pl.multiple_of(i * tq, tq), carry, causal)
    for c in range(heads):
        o_ref[:, c * Dv:(c + 1) * Dv] = carry[c][2] / carry[c][1]


def mla_attention(q, k, v):
    B, H, S, W = q.shape
    tq = min(TQ_MLA, S)
    nq = S // tq
    Dv = v.shape[-1]
    hp = MLA_HEADS_PER_STEP
    vmem = hp * (2 * (S * W * 2 + S * Dv * 2) + 4 * tq * W * 2 + 8 * tq * tq * 4)
    return pl.pallas_call(
        functools.partial(_mla_attn_kernel, tq=tq, heads=hp),
        grid=(B, H // hp, nq),
        in_specs=[pl.BlockSpec((1, hp, tq, W), lambda b, h, i: (b, h, i, 0)),
                  pl.BlockSpec((1, hp, S, W), lambda b, h, i: (b, h, 0, 0), pipeline_mode=pl.Buffered(1)),
                  pl.BlockSpec((1, hp, S, Dv), lambda b, h, i: (b, h, 0, 0), pipeline_mode=pl.Buffered(1))],
        out_specs=pl.BlockSpec((tq, hp * Dv), lambda b, h, i: (b * nq + i, h)),
        out_shape=jax.ShapeDtypeStruct((B * S, H * Dv), F32),
        compiler_params=_cparams(("parallel", "parallel", "arbitrary"), vmem),
        name="mla_attention",
    )(q, k, v)


def _slab_lanes(ref, c, rows, nchunk, row0=0):
    return ref[pl.ds(row0 * nchunk + c, rows, stride=nchunk), :]


def _slab_row_dma(idx_ref, base, src_hbm, dst_ref, dst_row0, sem, nchunk):
    def copy(r):
        src = src_hbm.at[pl.ds(pl.multiple_of(idx_ref[base + r] * nchunk, nchunk), nchunk)]
        dst = dst_ref.at[pl.ds(pl.multiple_of((dst_row0 + r) * nchunk, nchunk), nchunk)]
        return pltpu.make_async_copy(src, dst, sem)

    def start(r, c):
        copy(r).start()
        return c

    def wait(r, c):
        copy(r).wait()
        return c

    return start, wait


def _out_proj_kernel(x_ref, a_ref, b_ref, ga_ref, gb_ref, wa_ref, wb_ref, o_ref, *rest, slab_out):
    na_ref, nb_ref = rest[-2:]
    j = pl.program_id(1)

    @pl.when(j == 0)
    def _():
        na_ref[...] = _rms(a_ref[...], ga_ref[...]).astype(BF16)
        nb_ref[...] = _rms(b_ref[...], gb_ref[...]).astype(BF16)

    o = (x_ref[...] + jnp.dot(na_ref[...], wa_ref[...], preferred_element_type=F32)
         + jnp.dot(nb_ref[...], wb_ref[...], preferred_element_type=F32))
    o_ref[...] = o
    if slab_out:
        s_ref = rest[0]
        tm, tn = o.shape
        per = tn // LANES
        nchunk = s_ref.shape[0] // tm
        for jj in range(nchunk // per):
            @pl.when(j == jj)
            def _():
                for c in range(per):
                    s_ref[pl.ds(jj * per + c, tm, stride=nchunk), :] = o[:, c * LANES:(c + 1) * LANES]


def out_proj(x, o_nsa, o_mla, g_nsa, g_mla, w_out, slab_out):
    T, D = x.shape
    Ka, Kb = o_nsa.shape[1], o_mla.shape[1]
    assert Ka == Kb
    tm, tn = min(TM, T), TN_OUT
    nchunk = D // LANES
    vmem = 2 * (tm * tn * 8 + 2 * tm * Ka * 4 + 2 * Ka * tn * 2) + 2 * tm * Ka * 2
    out_specs = [pl.BlockSpec((tm, tn), lambda i, j: (i, j))]
    out_shape = [jax.ShapeDtypeStruct((T, D), F32)]
    if slab_out:
        out_specs.append(pl.BlockSpec((tm * nchunk, LANES), lambda i, j: (i, 0)))
        out_shape.append(jax.ShapeDtypeStruct((T * nchunk, LANES), F32))
        vmem += 2 * tm * D * 4
    return pl.pallas_call(
        functools.partial(_out_proj_kernel, slab_out=slab_out),
        grid=(T // tm, D // tn),
        in_specs=[pl.BlockSpec((tm, tn), lambda i, j: (i, j)),
                  pl.BlockSpec((tm, Ka), lambda i, j: (i, 0)),
                  pl.BlockSpec((tm, Kb), lambda i, j: (i, 0)),
                  pl.BlockSpec((1, Ka), lambda i, j: (0, 0)),
                  pl.BlockSpec((1, Kb), lambda i, j: (0, 0)),
                  pl.BlockSpec((Ka, tn), lambda i, j: (0, j)),
                  pl.BlockSpec((Kb, tn), lambda i, j: (1, j))],
        out_specs=out_specs,
        out_shape=out_shape,
        scratch_shapes=[pltpu.VMEM((tm, Ka), BF16), pltpu.VMEM((tm, Kb), BF16)],
        compiler_params=_cparams(("parallel", "arbitrary"), 2 * vmem),
        name="out_proj",
    )(x, o_nsa, o_mla, g_nsa, g_mla, w_out, w_out)


def _swiglu_kernel(*refs, gathered):
    if gathered:
        be_ref, tok_ref, x_hbm, g_ref, wg_ref, wu_ref, wd_ref, o_ref, h_ref, acc_ref, slab_ref, sem = refs
    else:
        be_ref, x_ref, g_ref, wg_ref, wu_ref, wd_ref, o_ref, h_ref, acc_ref = refs
    tm, D = h_ref.shape
    nchunk = D // LANES
    i = pl.program_id(0)
    f = pl.program_id(1)
    live = i < be_ref[0]

    @pl.when(live & (f == 0))
    def _():
        if gathered:
            start, wait = _slab_row_dma(tok_ref, i * tm, x_hbm, slab_ref, 0, sem, nchunk)
            lax.fori_loop(0, tm, start, 0, unroll=GATHER_UNROLL)
            lax.fori_loop(0, tm, wait, 0, unroll=GATHER_UNROLL)
            ss = jnp.zeros((tm, 1), F32)
            for c in range(nchunk):
                xc = _slab_lanes(slab_ref, c, tm, nchunk)
                ss = ss + jnp.sum(xc * xc, axis=-1, keepdims=True)
            r = lax.rsqrt(ss * (1.0 / D) + NORM_EPS)
            for c in range(nchunk):
                cs = slice(c * LANES, (c + 1) * LANES)
                h_ref[:, cs] = (_slab_lanes(slab_ref, c, tm, nchunk) * r * g_ref[:, cs]).astype(BF16)
        else:
            h_ref[...] = _rms(x_ref[...], g_ref[...]).astype(BF16)
        acc_ref[...] = jnp.zeros_like(acc_ref)

    @pl.when(live)
    def _():
        h = h_ref[...]
        a = jax.nn.silu(jnp.dot(h, wg_ref[0], preferred_element_type=F32)) * jnp.dot(
            h, wu_ref[0], preferred_element_type=F32)
        acc_ref[...] += jnp.dot(a.astype(BF16), wd_ref[0], preferred_element_type=F32)

    last = f == pl.num_programs(1) - 1

    @pl.when(live & last)
    def _():
        if gathered:
            for c in range(nchunk):
                o_ref[pl.ds(c, tm, stride=nchunk), :] = acc_ref[:, c * LANES:(c + 1) * LANES]
        else:
            o_ref[...] = x_ref[...] + acc_ref[...]

    if gathered:
        @pl.when(jnp.logical_not(live) & last)
        def _():
            o_ref[...] = jnp.zeros_like(o_ref)


def swiglu_blocks(x, g, wg, wu, wd, block_info, tm, row_tok=None):
    gathered = row_tok is not None
    D = wg.shape[1]
    F = wg.shape[2]
    nchunk = D // LANES
    R = row_tok.shape[0] if gathered else x.shape[0]
    tf = TF
    nf = F // tf
    nsp = 2 if gathered else 1

    def wcol(i, f, be, *_):
        live = i < be[0]
        return (be[1 + i], 0, jnp.where(live, f, nf - 1))

    def wrow(i, f, be, *_):
        live = i < be[0]
        return (be[1 + i], jnp.where(live, f, nf - 1), 0)

    w_specs = [pl.BlockSpec((1, D), lambda i, f, *_: (0, 0)),
               pl.BlockSpec((1, D, tf), wcol),
               pl.BlockSpec((1, D, tf), wcol),
               pl.BlockSpec((1, tf, D), wrow)]
    scratch = [pltpu.VMEM((tm, D), BF16), pltpu.VMEM((tm, D), F32)]
    vmem = 4 * tm * D * 4 + tm * D * 2 + tm * D * 4 + 2 * 3 * D * tf * 2 + 6 * tm * tf * 4
    if gathered:
        in_specs = [pl.BlockSpec(memory_space=pl.ANY)] + w_specs
        out_spec = pl.BlockSpec((tm * nchunk, LANES), lambda i, f, *_: (i, 0))
        out_shape = jax.ShapeDtypeStruct((R * nchunk, LANES), F32)
        scratch += [pltpu.VMEM((tm * nchunk, LANES), F32), pltpu.SemaphoreType.DMA(())]
        prefetch = (block_info, row_tok)
    else:
        in_specs = [pl.BlockSpec((tm, D), lambda i, f, *_: (i, 0))] + w_specs
        out_spec = pl.BlockSpec((tm, D), lambda i, f, *_: (i, 0))
        out_shape = jax.ShapeDtypeStruct((R, D), F32)
        prefetch = (block_info,)
    grid_spec = pltpu.PrefetchScalarGridSpec(
        num_scalar_prefetch=nsp,
        grid=(R // tm, nf),
        in_specs=in_specs,
        out_specs=out_spec,
        scratch_shapes=scratch,
    )
    return pl.pallas_call(
        functools.partial(_swiglu_kernel, gathered=gathered),
        grid_spec=grid_spec,
        out_shape=out_shape,
        compiler_params=_cparams(("parallel", "arbitrary"), vmem + (4 << 20)),
        name="swiglu",
    )(*prefetch, x, g, wg, wu, wd)


def _router_kernel(x_ref, g_ref, w_ref, o_ref):
    h = _rms(x_ref[...], g_ref[...]).astype(BF16)
    logits = jnp.dot(h, w_ref[...], preferred_element_type=F32)
    lane = lax.broadcasted_iota(jnp.int32, logits.shape, 1)
    lf = lane.astype(F32)
    lg = jnp.where(lane < N_EXPERTS, logits, -jnp.inf)
    m1 = jnp.max(lg, axis=-1, keepdims=True)
    i1 = jnp.min(jnp.where(lg == m1, lf, float(LANES)), axis=-1, keepdims=True)
    lg2 = jnp.where(lf == i1, -jnp.inf, lg)
    m2 = jnp.max(lg2, axis=-1, keepdims=True)
    i2 = jnp.min(jnp.where(lg2 == m2, lf, float(LANES)), axis=-1, keepdims=True)
    e2 = jnp.exp(m2 - m1)
    den = 1.0 + e2
    o_ref[...] = jnp.where(lane == 0, i1, jnp.where(lane == 1, i2, jnp.where(lane == 2, 1.0 / den, e2 / den)))


def moe_router(x, g, w_router_p):
    T, D = x.shape
    tm = min(TM, T)
    return pl.pallas_call(
        _router_kernel,
        grid=(T // tm,),
        in_specs=[pl.BlockSpec((tm, D), lambda i: (i, 0)),
                  pl.BlockSpec((1, D), lambda i: (0, 0)),
                  pl.BlockSpec((D, LANES), lambda i: (0, 0))],
        out_specs=pl.BlockSpec((tm, LANES), lambda i: (i, 0)),
        out_shape=jax.ShapeDtypeStruct((T, LANES), F32),
        compiler_params=_cparams(("parallel",), 4 * tm * D * 4),
        name="moe_router",
    )(x, g, w_router_p)


def _combine_kernel(idx_ref, x_ref, y_hbm, gate_ref, o_ref, slab_ref, sem):
    tm, D = x_ref.shape
    nchunk = D // LANES
    base = pl.program_id(0) * tm
    n_tok = idx_ref.shape[0] // TOP_K
    for k in range(TOP_K):
        start, _ = _slab_row_dma(idx_ref, k * n_tok + base, y_hbm, slab_ref, k * tm, sem, nchunk)
        lax.fori_loop(0, tm, start, 0, unroll=GATHER_UNROLL)
    for k in range(TOP_K):
        _, wait = _slab_row_dma(idx_ref, k * n_tok + base, y_hbm, slab_ref, k * tm, sem, nchunk)
        lax.fori_loop(0, tm, wait, 0, unroll=GATHER_UNROLL)
    gate = gate_ref[...]
    for c in range(nchunk):
        cs = slice(c * LANES, (c + 1) * LANES)
        o_ref[:, cs] = x_ref[:, cs] + (_slab_lanes(slab_ref, c, tm, nchunk) * gate[:, 2:3]
                                       + _slab_lanes(slab_ref, c, tm, nchunk, row0=tm) * gate[:, 3:4])


def moe_combine(x, y_slab, pair_idx, route):
    T, D = x.shape
    tm = min(COMBINE_TM, T)
    nchunk = D // LANES
    grid_spec = pltpu.PrefetchScalarGridSpec(
        num_scalar_prefetch=1,
        grid=(T // tm,),
        in_specs=[pl.BlockSpec((tm, D), lambda i, idx: (i, 0)),
                  pl.BlockSpec(memory_space=pl.ANY),
                  pl.BlockSpec((tm, LANES), lambda i, idx: (i, 0))],
        out_specs=pl.BlockSpec((tm, D), lambda i, idx: (i, 0)),
        scratch_shapes=[pltpu.VMEM((TOP_K * tm * nchunk, LANES), F32), pltpu.SemaphoreType.DMA(())],
    )
    return pl.pallas_call(
        _combine_kernel,
        grid_spec=grid_spec,
        out_shape=jax.ShapeDtypeStruct((T, D), F32),
        compiler_params=_cparams(("arbitrary",), 8 * tm * D * 4),
        name="moe_combine",
    )(pair_idx, x, y_slab, route)


def moe_ffn(x, x_slab, g, w_router_p, wg, wu, wd, expert_base):
    T, D = x.shape
    n_assign = T * TOP_K
    route = moe_router(x, g, w_router_p)
    flat_e = route[:, :TOP_K].astype(jnp.int32).reshape(-1)
    order = jnp.argsort(flat_e)
    se = flat_e[order]
    stok = (order // TOP_K).astype(jnp.int32)
    counts = jnp.bincount(flat_e, length=N_EXPERTS)
    padded = (counts + MOE_TM - 1) // MOE_TM * MOE_TM
    pad_end = jnp.cumsum(padded)
    pad_start = pad_end - padded
    start = jnp.cumsum(counts) - counts
    dest = (pad_start[se] + jnp.arange(n_assign) - start[se]).astype(jnp.int32)
    n_blocks = -(-(n_assign + N_EXPERTS * (MOE_TM - 1)) // MOE_TM)
    n_rows = n_blocks * MOE_TM
    block_e = jnp.minimum(jnp.searchsorted(pad_end, jnp.arange(n_blocks) * MOE_TM, side='right'), N_EXPERTS - 1)
    n_live = (pad_end[-1] // MOE_TM).astype(jnp.int32)
    block_info = jnp.concatenate([n_live[None], block_e.astype(jnp.int32) + expert_base])
    row_e = jnp.repeat(block_e, MOE_TM)
    row_j = jnp.arange(n_rows) - pad_start[row_e]
    row_src = jnp.clip(start[row_e] + row_j, 0, n_assign - 1)
    row_tok = jnp.where(row_j < counts[row_e], stok[row_src], 0).astype(jnp.int32)
    dest_flat = dest[jnp.argsort(order)]
    pair_idx = jnp.concatenate([dest_flat[0::2], dest_flat[1::2]])

    y_slab = swiglu_blocks(x_slab, g, wg, wu, wd, block_info, MOE_TM, row_tok=row_tok)
    return moe_combine(x, y_slab, pair_idx, route)


def _rope_tab(pos, rot_dim):
    half = rot_dim // 2
    inv_freq = jnp.power(jnp.float32(ROPE_THETA), -jnp.arange(0, rot_dim, 2, dtype=F32) / rot_dim)
    ang = pos.astype(F32)[..., None] * inv_freq
    cos, sin = jnp.cos(ang), jnp.sin(ang)
    rest = LANES - rot_dim
    one = jnp.ones(pos.shape + (rest,), F32)
    zr = jnp.zeros(pos.shape + (rest,), F32)
    zh = jnp.zeros_like(sin)
    return jnp.stack([jnp.concatenate([cos, cos, one], -1),
                      jnp.concatenate([zh, sin, zr], -1),
                      jnp.concatenate([-sin, zh, zr], -1)])


def _pad_cols(w, n):
    return jnp.pad(w, ((0, 0), (0, n - w.shape[1])))


def kernel(x, positions, attn_norm, w_in, nsa_q_norm, nsa_k_norm, cmp_k_w1, cmp_k_w2, cmp_k_pos, cmp_v_w1, cmp_v_w2, cmp_v_pos, mla_q_a_norm, mla_w_q_up, mla_kv_a_norm, mla_w_kv_up, mla_q_norm, mla_k_norm, nsa_out_norm, mla_out_norm, w_out, ffn_norm, dense_w_gate, dense_w_up, dense_w_down, moe_router, moe_w_gate, moe_w_up, moe_w_down):
    B, S, D = x.shape
    T = B * S
    depth = w_in.shape[0]
    Dh = NSA_HEAD_DIM
    ncp = S // CMP_STRIDE
    kw = CMP_STRIDE * Dh

    rope_n = _rope_tab(positions, NSA_ROT_DIM).reshape(3, T, LANES)
    rope_m = _rope_tab(positions, MLA_ROPE_DIM).reshape(3, T, LANES)
    cmp_idx = jnp.minimum(jnp.arange(ncp) * CMP_STRIDE + CMP_BLOCK - 1, S - 1)
    rope_c = _rope_tab(positions[:, cmp_idx], NSA_ROT_DIM)

    dense_g, dense_u, dense_d = (w.astype(BF16) for w in (dense_w_gate, dense_w_up, dense_w_down))
    moe_g, moe_u, moe_d = (w.reshape((-1,) + w.shape[2:]).astype(BF16) for w in (moe_w_gate, moe_w_up, moe_w_down))

    g0 = NSA_Q_COLS + NSA_KV_COLS
    m0 = g0 + NSA_GATE_COLS
    xt = x.reshape(T, D)
    for layer in range(depth):
        w = w_in[layer]
        w_in_p = jnp.concatenate([w[:, :g0], w[:, m0:m0 + MLA_Q_RANK], w[:, m0 + MLA_Q_RANK:m0 + MLA_Q_RANK + MLA_KV_RANK],
                                  _pad_cols(w[:, m0 + MLA_Q_RANK + MLA_KV_RANK:], LANES),
                                  _pad_cols(w[:, g0:m0], LANES)], axis=1).astype(BF16)
        proj = rms_matmul(xt, attn_norm[layer][None], w_in_p, min(TM, T), TN_PROJ)

        gk = nsa_k_norm[layer]
        q_n, kc_raw, vc_raw, ks, vs, kwin, vwin = nsa_prep(proj, rope_n, nsa_q_norm[layer][None], gk, B, S)
        kc, vc = nsa_compress(
            kc_raw, vc_raw,
            cmp_k_w1[layer].reshape(2, kw, Dh).astype(BF16), cmp_k_w2[layer].astype(BF16),
            cmp_k_pos[layer].reshape(2, 1, kw),
            cmp_v_w1[layer].reshape(2, kw, Dh).astype(BF16), cmp_v_w2[layer].astype(BF16),
            cmp_v_pos[layer].reshape(2, 1, kw), rope_c, gk)
        o_nsa = nsa_attention(q_n, proj, kc, vc, ks, vs, kwin, vwin, B, S)

        wq = jnp.pad(mla_w_q_up[layer].reshape(MLA_Q_RANK, MLA_HEADS, MLA_QK_DIM),
                     ((0, 0), (0, 0), (0, MLA_QK_PAD - MLA_QK_DIM))).reshape(MLA_Q_RANK, MLA_HEADS * MLA_QK_PAD)
        gq = jnp.pad(mla_q_norm[layer], (0, MLA_QK_PAD - MLA_QK_DIM))[None]
        gkm = jnp.pad(mla_k_norm[layer], (0, MLA_QK_PAD - MLA_QK_DIM))[None]
        q_m = mla_q_prep(proj, mla_q_a_norm[layer][None], wq.astype(BF16), gq, rope_m, B, S)
        k_m, v_m = mla_kv_prep(proj, mla_kv_a_norm[layer][None], mla_w_kv_up[layer].astype(BF16), gkm, rope_m, B, S)
        o_mla = mla_attention(q_m, k_m, v_m)

        is_moe = layer % 2 == 1
        mixed = out_proj(xt, o_nsa, o_mla, nsa_out_norm[layer][None], mla_out_norm[layer][None],
                         w_out[layer].astype(BF16), slab_out=is_moe)

        i = layer // 2
        gf = ffn_norm[layer][None]
        if is_moe:
            xt, xt_slab = mixed
            xt = moe_ffn(xt, xt_slab, gf, _pad_cols(moe_router[i], LANES).astype(BF16), moe_g, moe_u, moe_d,
                         i * N_EXPERTS)
        else:
            xt, = mixed
            tm = min(TM, T)
            info = jnp.concatenate([jnp.full((1,), T // tm, jnp.int32), jnp.full((T // tm,), i, jnp.int32)])
            xt = swiglu_blocks(xt, gf, dense_g, dense_u, dense_d, info, tm)
    return xt.reshape(B, S, D)
```

```python
import functools

import jax
import jax.numpy as jnp
from jax import lax
from jax.experimental import pallas as pl
from jax.experimental.pallas import tpu as pltpu

F32 = jnp.float32
BF16 = jnp.bfloat16

LANES = 128
VMEM_CAP_BYTES = 56 * 1024 * 1024

LOG2E = 1.4426950408889634
NEG = -1e30

NORM_EPS = 1e-6
ROPE_THETA = 500000.0
Q_BLOCK = 128

NSA_HEAD_DIM = 128
NSA_KV_GROUPS = 2
NSA_HEADS_PER_GROUP = 4
NSA_HEADS = NSA_KV_GROUPS * NSA_HEADS_PER_GROUP
NSA_ROT_DIM = NSA_HEAD_DIM // 4
N_NSA_BRANCHES = 3
CMP_BLOCK = 32
CMP_STRIDE = 16
SEL_BLOCK = 64
SEL_SHIFT = SEL_BLOCK.bit_length() - 1
assert 1 << SEL_SHIFT == SEL_BLOCK
N_SEL_BLOCKS = 16
WINDOW = 512
SEL_FORCE = 1.0e4

MLA_V_DIM = 128
MLA_HEADS = 8
MLA_Q_RANK = 512
MLA_KV_RANK = 512
MLA_NOPE_DIM = 128
MLA_ROPE_DIM = 64
MLA_QK_DIM = MLA_NOPE_DIM + MLA_ROPE_DIM
MLA_QK_PAD = 256

N_EXPERTS = 8
TOP_K = 2

NSA_Q_COLS = NSA_HEADS * NSA_HEAD_DIM
NSA_KV_COLS = N_NSA_BRANCHES * 2 * NSA_KV_GROUPS * NSA_HEAD_DIM
NSA_GATE_COLS = NSA_HEADS * N_NSA_BRANCHES
P_Q0 = 0
P_KV0 = P_Q0 + NSA_Q_COLS
P_QA0 = P_KV0 + NSA_KV_COLS
P_KVA0 = P_QA0 + MLA_Q_RANK
P_KR0 = P_KVA0 + MLA_KV_RANK
P_G0 = P_KR0 + LANES
P_COLS = P_G0 + LANES

TM = 512
TN_PROJ = 1280
TN_OUT = 1024
TF = 512
TS_PREP = 1024
TK_ATT = 512
TQ_MLA = 512
MLA_HEADS_PER_STEP = 4
MOE_TM = 512
COMBINE_TM = 256
GATHER_UNROLL = 8


def _cparams(sem, vmem_bytes):
    return pltpu.CompilerParams(dimension_semantics=sem,
                                vmem_limit_bytes=int(min(VMEM_CAP_BYTES, max(vmem_bytes, 16 * 1024 * 1024))))


def _rms(x, g):
    ms = jnp.mean(x * x, axis=-1, keepdims=True)
    return x * lax.rsqrt(ms + NORM_EPS) * g


def _rope_lanes(x, tab, half):
    return x * tab[0] + pltpu.roll(x, half, 1) * tab[1] + pltpu.roll(x, LANES - half, 1) * tab[2]


def _dot_nt(a, b):
    return lax.dot_general(a, b, (((1,), (1,)), ((), ())), preferred_element_type=F32)


def _rms_matmul_kernel(x_ref, g_ref, w_ref, o_ref, h_ref):
    @pl.when(pl.program_id(1) == 0)
    def _():
        h_ref[...] = _rms(x_ref[...], g_ref[...]).astype(BF16)

    o_ref[...] = jnp.dot(h_ref[...], w_ref[...], preferred_element_type=F32)


def rms_matmul(x, g, w, tm, tn):
    T, K = x.shape
    N = w.shape[1]
    vmem = 2 * tm * K * 4 + tm * K * 2 + 2 * K * tn * 2 + 2 * tm * tn * 4
    return pl.pallas_call(
        _rms_matmul_kernel,
        grid=(T // tm, N // tn),
        in_specs=[pl.BlockSpec((tm, K), lambda i, j: (i, 0)),
                  pl.BlockSpec((1, K), lambda i, j: (0, 0)),
                  pl.BlockSpec((K, tn), lambda i, j: (0, j))],
        out_specs=pl.BlockSpec((tm, tn), lambda i, j: (i, j)),
        out_shape=jax.ShapeDtypeStruct((T, N), F32),
        scratch_shapes=[pltpu.VMEM((tm, K), BF16)],
        compiler_params=_cparams(("parallel", "arbitrary"), 2 * vmem),
        name="rms_matmul",
    )(x, g, w)


def _nsa_prep_kernel(q_in, kc_in, vc_in, ks_in, vs_in, kw_in, vw_in, rope_ref, gq_ref, gk_ref,
                     q_out, kc_out, vc_out, ks_out, vs_out, kw_out, vw_out):
    tab = rope_ref[...]
    half = NSA_ROT_DIM // 2
    Dh = NSA_HEAD_DIM
    fold = Dh ** -0.5 * LOG2E
    for h in range(NSA_HEADS_PER_GROUP):
        qh = _rope_lanes(_rms(q_in[:, h * Dh:(h + 1) * Dh], gq_ref[...]), tab, half)
        q_out[0, 0, h] = (qh * fold).astype(BF16)
    kc_out[0, 0] = kc_in[...]
    vc_out[0, 0] = vc_in[...]
    ks_out[0, 0] = _rope_lanes(_rms(ks_in[...], gk_ref[1:2, :]), tab, half).astype(BF16)
    kw_out[0, 0] = _rope_lanes(_rms(kw_in[...], gk_ref[2:3, :]), tab, half).astype(BF16)
    vs_out[0, 0] = vs_in[...].astype(BF16)
    vw_out[0, 0] = vw_in[...].astype(BF16)


def nsa_prep(proj, rope_n, gq, gk, B, S):
    ts = min(TS_PREP, S)
    ns = S // ts
    G, Hg, Dh = NSA_KV_GROUPS, NSA_HEADS_PER_GROUP, NSA_HEAD_DIM
    kv_blk0 = P_KV0 // Dh

    def in_spec(branch, kv):
        off = kv_blk0 + (branch * 2 + kv) * G
        return pl.BlockSpec((ts, Dh), lambda b, s, g: (b * ns + s, off + g))

    out_spec = pl.BlockSpec((1, 1, ts, Dh), lambda b, s, g: (b, g, s, 0))
    f32_out = jax.ShapeDtypeStruct((B, G, S, Dh), F32)
    bf_out = jax.ShapeDtypeStruct((B, G, S, Dh), BF16)
    return pl.pallas_call(
        _nsa_prep_kernel,
        grid=(B, ns, G),
        in_specs=[pl.BlockSpec((ts, Hg * Dh), lambda b, s, g: (b * ns + s, P_Q0 // (Hg * Dh) + g)),
                  in_spec(0, 0), in_spec(0, 1), in_spec(1, 0), in_spec(1, 1), in_spec(2, 0), in_spec(2, 1),
                  pl.BlockSpec((3, ts, Dh), lambda b, s, g: (0, b * ns + s, 0)),
                  pl.BlockSpec((1, Dh), lambda b, s, g: (0, 0)),
                  pl.BlockSpec((N_NSA_BRANCHES, Dh), lambda b, s, g: (0, 0))],
        out_specs=[pl.BlockSpec((1, 1, Hg, ts, Dh), lambda b, s, g: (b, g, 0, s, 0))] + [out_spec] * 6,
        out_shape=[jax.ShapeDtypeStruct((B, G, Hg, S, Dh), BF16), f32_out, f32_out, bf_out, bf_out, bf_out, bf_out],
        compiler_params=_cparams(("parallel", "parallel", "parallel"), 2 * 2 * 20 * ts * Dh * 4),
        name="nsa_prep",
    )(proj, proj, proj, proj, proj, proj, proj, rope_n, gq, gk)


def _compress_one(t_ref, w1_ref, w2_ref, pos_ref):
    t = t_ref[0, 0]
    ncp = t.shape[0]
    u = jnp.dot((t + pos_ref[0]).astype(BF16), w1_ref[0], preferred_element_type=F32)
    v = jnp.dot((t + pos_ref[1]).astype(BF16), w1_ref[1], preferred_element_type=F32)
    pre = u + pltpu.roll(v, ncp - 1, 0)
    return jnp.dot(jax.nn.gelu(pre).astype(BF16), w2_ref[...], preferred_element_type=F32)


def _compress_kernel(tk_ref, tv_ref, w1k_ref, w2k_ref, posk_ref, w1v_ref, w2v_ref, posv_ref, rope_ref, gk_ref,
                     kc_ref, vc_ref):
    k = _compress_one(tk_ref, w1k_ref, w2k_ref, posk_ref)
    kc_ref[0, 0] = _rope_lanes(_rms(k, gk_ref[0:1, :]), rope_ref[:, 0], NSA_ROT_DIM // 2).astype(BF16)
    vc_ref[0, 0] = _compress_one(tv_ref, w1v_ref, w2v_ref, posv_ref).astype(BF16)


def nsa_compress(kc_raw, vc_raw, w1k, w2k, posk, w1v, w2v, posv, rope_c, gk):
    B, G, S, Dh = kc_raw.shape
    ncp = S // CMP_STRIDE
    kw = CMP_STRIDE * Dh
    tk = kc_raw.reshape(B, G, ncp, kw)
    tv = vc_raw.reshape(B, G, ncp, kw)
    t_spec = pl.BlockSpec((1, 1, ncp, kw), lambda b, g: (b, g, 0, 0))
    w1_spec = pl.BlockSpec((2, kw, Dh), lambda b, g: (0, 0, 0))
    w2_spec = pl.BlockSpec((Dh, Dh), lambda b, g: (0, 0))
    pos_spec = pl.BlockSpec((2, 1, kw), lambda b, g: (0, 0, 0))
    out_spec = pl.BlockSpec((1, 1, ncp, Dh), lambda b, g: (b, g, 0, 0))
    out = jax.ShapeDtypeStruct((B, G, ncp, Dh), BF16)
    return pl.pallas_call(
        _compress_kernel,
        grid=(B, G),
        in_specs=[t_spec, t_spec, w1_spec, w2_spec, pos_spec, w1_spec, w2_spec, pos_spec,
                  pl.BlockSpec((3, 1, ncp, Dh), lambda b, g: (0, b, 0, 0)),
                  pl.BlockSpec((N_NSA_BRANCHES, Dh), lambda b, g: (0, 0))],
        out_specs=[out_spec, out_spec],
        out_shape=[out, out],
        compiler_params=_cparams(("parallel", "parallel"), 2 * (4 * ncp * kw * 4 + 8 * kw * Dh * 2)),
        name="nsa_compress",
    )(tk, tv, w1k, w2k, posk, w1v, w2v, posv, rope_c, gk)


def _online_softmax(s, m, l):
    m_new = jnp.maximum(m, jnp.max(s, axis=-1, keepdims=True))
    alpha = jnp.exp2(m - m_new)
    p = jnp.exp2(s - m_new)
    return p, m_new, alpha * l + jnp.sum(p, axis=-1, keepdims=True), alpha


def _softmax_rows(s):
    p = jnp.exp2(s - jnp.max(s, axis=-1, keepdims=True))
    return p, jnp.sum(p, axis=-1, keepdims=True)


def _nsa_attn_kernel(q_ref, gl_ref, kc_ref, vc_ref, ks_ref, vs_ref, kw_ref, vw_ref, o_ref,
                     *, seq, n_sel):
    G, Hg, Dh, QB = NSA_KV_GROUPS, NSA_HEADS_PER_GROUP, NSA_HEAD_DIM, Q_BLOCK
    R = Hg * QB
    s0 = pl.program_id(1) * QB
    ncp = kc_ref.shape[2]
    nb = max(LANES, seq // SEL_BLOCK)
    tk = TK_ATT
    wk = WINDOW + QB

    t_q = s0 + lax.broadcasted_iota(jnp.int32, (QB, 1), 0)
    t_l = s0 + lax.broadcasted_iota(jnp.int32, (1, QB), 1)

    cmp_end = lax.broadcasted_iota(jnp.int32, (1, ncp), 1) * CMP_STRIDE + (CMP_BLOCK - 1)
    bias_c = jnp.where(cmp_end <= t_q, 0.0, NEG)
    row_ok = jnp.where(t_q >= CMP_BLOCK - 1, 1.0, 0.0)
    w0 = pl.multiple_of(jnp.maximum(s0 - WINDOW, 0), QB)
    diff = t_q - (w0 + lax.broadcasted_iota(jnp.int32, (1, wk), 1))
    bias_w = jnp.where((diff >= 0) & (diff < WINDOW), 0.0, NEG)

    j_i = lax.broadcasted_iota(jnp.int32, (nb, ncp), 0) * SEL_BLOCK
    c_i = lax.broadcasted_iota(jnp.int32, (nb, ncp), 1) * CMP_STRIDE
    overlap_t = jnp.where((c_i < j_i + SEL_BLOCK) & (c_i + (CMP_BLOCK - 1) >= j_i), 1.0, 0.0).astype(BF16)
    jt = lax.broadcasted_iota(jnp.int32, (nb, QB), 0)
    cur = jnp.right_shift(t_l, SEL_SHIFT)
    force_t = jnp.where((jt == 0) | (jt == cur) | (jt == cur - 1), SEL_FORCE, 0.0)
    eligible_t = jt <= cur
    jf = jt.astype(F32)

    qbs, sels, o_cs, o_ws = [], [], [], []
    for g in range(G):
        qb = q_ref[0, g].reshape(R, Dh)
        qbs.append(qb)

        p, l = _softmax_rows(_dot_nt(qb, kc_ref[0, g]).reshape(Hg, QB, ncp) + bias_c[None])
        p = p * (row_ok[None] / l)
        o_cs.append(jnp.dot(p.reshape(R, ncp).astype(BF16), vc_ref[0, g], preferred_element_type=F32))

        imp = _dot_nt(overlap_t, jnp.sum(p, axis=0).astype(BF16))
        imp = jnp.where(eligible_t, imp + force_t, -jnp.inf)
        sel = jnp.zeros((nb, QB), F32)
        for _ in range(n_sel):
            mx = jnp.max(imp, axis=0, keepdims=True)
            first = jnp.min(jnp.where(imp == mx, jf, float(nb)), axis=0, keepdims=True)
            pick = (jf == first) & (mx > -jnp.inf)
            sel = jnp.where(pick, 1.0, sel)
            imp = jnp.where(pick, -jnp.inf, imp)
        sels.append(sel.T.astype(BF16))

        p, l = _softmax_rows(_dot_nt(qb, kw_ref[0, g, pl.ds(w0, wk), :]).reshape(Hg, QB, wk) + bias_w[None])
        o_w = jnp.dot(p.reshape(R, wk).astype(BF16), vw_ref[0, g, pl.ds(w0, wk), :], preferred_element_type=F32)
        o_ws.append(o_w / l.reshape(R, 1))

    jb = lax.broadcasted_iota(jnp.int32, (nb, tk), 0)
    kl = lax.broadcasted_iota(jnp.int32, (nb, tk), 1)
    kl_row = lax.broadcasted_iota(jnp.int32, (1, tk), 1)

    def sel_step(kt, carry):
        k0 = pl.multiple_of(kt * tk, tk)
        expand = jnp.where(jb == jnp.right_shift(k0 + kl, SEL_SHIFT), 1.0, 0.0).astype(BF16)
        causal = k0 + kl_row <= t_q
        out = []
        for g in range(G):
            m, l, acc = carry[g]
            chosen = jnp.dot(sels[g], expand, preferred_element_type=F32)
            bias = jnp.where((chosen > 0.5) & causal, 0.0, NEG)
            s = _dot_nt(qbs[g], ks_ref[0, g, pl.ds(k0, tk), :]).reshape(Hg, QB, tk) + bias[None]
            p, m, l, alpha = _online_softmax(s, m, l)
            pv = jnp.dot(p.reshape(R, tk).astype(BF16), vs_ref[0, g, pl.ds(k0, tk), :], preferred_element_type=F32)
            out.append((m, l, alpha * acc + pv.reshape(Hg, QB, Dh)))
        return tuple(out)

    init = tuple((jnp.full((Hg, QB, 1), NEG, F32), jnp.zeros((Hg, QB, 1), F32), jnp.zeros((Hg, QB, Dh), F32))
                 for _ in range(G))
    final = lax.fori_loop(0, (s0 + QB + tk - 1) // tk, sel_step, init)

    gates = jax.nn.sigmoid(gl_ref[...])
    for g in range(G):
        o_s = (final[g][2] / final[g][1]).reshape(R, Dh)
        for h in range(Hg):
            c0 = (g * Hg + h) * N_NSA_BRANCHES
            rs = slice(h * QB, (h + 1) * QB)
            o_ref[:, (g * Hg + h) * Dh:(g * Hg + h + 1) * Dh] = (
                gates[:, c0:c0 + 1] * o_cs[g][rs] + gates[:, c0 + 1:c0 + 2] * o_s[rs]
                + gates[:, c0 + 2:c0 + 3] * o_ws[g][rs])


def nsa_attention(q, proj, kc, vc, ks, vs, kw, vw, B, S):
    G, Hg, Dh, QB = NSA_KV_GROUPS, NSA_HEADS_PER_GROUP, NSA_HEAD_DIM, Q_BLOCK
    nq = S // QB
    ncp = kc.shape[2]
    n_sel = min(N_SEL_BLOCKS, S // SEL_BLOCK)
    qcols = G * Hg * Dh
    kv_spec = pl.BlockSpec((1, G, S, Dh), lambda b, i: (b, 0, 0, 0), pipeline_mode=pl.Buffered(1))
    c_spec = pl.BlockSpec((1, G, ncp, Dh), lambda b, i: (b, 0, 0, 0), pipeline_mode=pl.Buffered(1))
    sw = max(TK_ATT, ncp, WINDOW + QB)
    vmem = G * (4 * S * Dh * 2 + 2 * ncp * Dh * 2) + 10 * G * Hg * QB * sw * 4
    return pl.pallas_call(
        functools.partial(_nsa_attn_kernel, seq=S, n_sel=n_sel),
        grid=(B, nq),
        in_specs=[pl.BlockSpec((1, G, Hg, QB, Dh), lambda b, i: (b, 0, 0, i, 0)),
                  pl.BlockSpec((QB, LANES), lambda b, i: (b * nq + i, P_G0 // LANES)),
                  c_spec, c_spec, kv_spec, kv_spec, kv_spec, kv_spec],
        out_specs=pl.BlockSpec((QB, qcols), lambda b, i: (b * nq + i, 0)),
        out_shape=jax.ShapeDtypeStruct((B * S, qcols), F32),
        compiler_params=_cparams(("parallel", "arbitrary"), vmem),
        name="nsa_attention",
    )(q, proj, kc, vc, ks, vs, kw, vw)


def _mla_q_kernel(qa_ref, ga_ref, w_ref, gq_ref, rope_ref, o_ref):
    W = MLA_QK_PAD
    c = _rms(qa_ref[...], ga_ref[...]).astype(BF16)
    q = jnp.dot(c, w_ref[...], preferred_element_type=F32)
    tab = rope_ref[...]
    fold = MLA_QK_DIM ** -0.5 * LOG2E
    for h in range(MLA_HEADS):
        qh = q[:, h * W:(h + 1) * W]
        ms = jnp.sum(qh * qh, axis=-1, keepdims=True) * (1.0 / MLA_QK_DIM)
        qn = qh * lax.rsqrt(ms + NORM_EPS) * gq_ref[...]
        o_ref[0, h, :, :LANES] = (qn[:, :LANES] * fold).astype(BF16)
        o_ref[0, h, :, LANES:] = (_rope_lanes(qn[:, LANES:], tab, MLA_ROPE_DIM // 2) * fold).astype(BF16)


def mla_q_prep(proj, ga, wq, gq, rope_m, B, S):
    tm = min(TM, S)
    ns = S // tm
    H, R, W = MLA_HEADS, MLA_Q_RANK, MLA_QK_PAD
    return pl.pallas_call(
        _mla_q_kernel,
        grid=(B * ns,),
        in_specs=[pl.BlockSpec((tm, R), lambda r: (r, P_QA0 // R)),
                  pl.BlockSpec((1, R), lambda r: (0, 0)),
                  pl.BlockSpec((R, H * W), lambda r: (0, 0)),
                  pl.BlockSpec((1, W), lambda r: (0, 0)),
                  pl.BlockSpec((3, tm, LANES), lambda r: (0, r, 0))],
        out_specs=pl.BlockSpec((1, H, tm, W), lambda r: (r // ns, 0, r % ns, 0)),
        out_shape=jax.ShapeDtypeStruct((B, H, S, W), BF16),
        compiler_params=_cparams(("parallel",), 2 * (tm * R * 4 + R * H * W * 2 + tm * H * W * 2) + 3 * tm * H * W * 4),
        name="mla_q_prep",
    )(proj, ga, wq, gq, rope_m)


def _mla_kv_kernel(kva_ref, kr_ref, ga_ref, w_ref, gk_ref, rope_ref, k_ref, v_ref):
    W = MLA_QK_PAD
    c = _rms(kva_ref[...], ga_ref[...]).astype(BF16)
    kv = jnp.dot(c, w_ref[...], preferred_element_type=F32)
    tab = rope_ref[...]
    k_rot = kr_ref[...]
    ss_rot = jnp.sum(k_rot * k_rot, axis=-1, keepdims=True)
    for h in range(MLA_HEADS):
        k_nope = kv[:, h * W:h * W + LANES]
        ms = (jnp.sum(k_nope * k_nope, axis=-1, keepdims=True) + ss_rot) * (1.0 / MLA_QK_DIM)
        r = lax.rsqrt(ms + NORM_EPS)
        k_ref[0, h, :, :LANES] = (k_nope * r * gk_ref[:, :LANES]).astype(BF16)
        k_ref[0, h, :, LANES:] = _rope_lanes(k_rot * r * gk_ref[:, LANES:], tab, MLA_ROPE_DIM // 2).astype(BF16)
        v_ref[0, h] = kv[:, h * W + LANES:(h + 1) * W].astype(BF16)


def mla_kv_prep(proj, ga, wkv, gk, rope_m, B, S):
    tm = min(TM, S)
    ns = S // tm
    H, R, W = MLA_HEADS, MLA_KV_RANK, MLA_QK_PAD
    return pl.pallas_call(
        _mla_kv_kernel,
        grid=(B * ns,),
        in_specs=[pl.BlockSpec((tm, R), lambda r: (r, P_KVA0 // R)),
                  pl.BlockSpec((tm, LANES), lambda r: (r, P_KR0 // LANES)),
                  pl.BlockSpec((1, R), lambda r: (0, 0)),
                  pl.BlockSpec((R, H * W), lambda r: (0, 0)),
                  pl.BlockSpec((1, W), lambda r: (0, 0)),
                  pl.BlockSpec((3, tm, LANES), lambda r: (0, r, 0))],
        out_specs=[pl.BlockSpec((1, H, tm, W), lambda r: (r // ns, 0, r % ns, 0)),
                   pl.BlockSpec((1, H, tm, MLA_V_DIM), lambda r: (r // ns, 0, r % ns, 0))],
        out_shape=[jax.ShapeDtypeStruct((B, H, S, W), BF16), jax.ShapeDtypeStruct((B, H, S, MLA_V_DIM), BF16)],
        compiler_params=_cparams(("parallel",), 2 * (tm * R * 4 + R * H * W * 2 + tm * H * W * 3) + 3 * tm * H * W * 4),
        name="mla_kv_prep",
    )(proj, proj, ga, wkv, gk, rope_m)


def _mla_attn_kernel(q_ref, k_ref, v_ref, o_ref, *, tq, heads):
    i = pl.program_id(2)
    Dv = v_ref.shape[-1]

    def tile(k0, carry, bias):
        out = []
        for c in range(heads):
            m, l, acc = carry[c]
            s = _dot_nt(q_ref[0, c], k_ref[0, c, pl.ds(k0, tq), :])
            if bias is not None:
                s = s + bias
            p, m, l, alpha = _online_softmax(s, m, l)
            pv = jnp.dot(p.astype(BF16), v_ref[0, c, pl.ds(k0, tq), :], preferred_element_type=F32)
            out.append((m, l, alpha * acc + pv))
        return tuple(out)

    init = tuple((jnp.full((tq, 1), NEG, F32), jnp.zeros((tq, 1), F32), jnp.zeros((tq, Dv), F32))
                 for _ in range(heads))
    carry = lax.fori_loop(0, i, lambda kt, c: tile(pl.multiple_of(kt * tq, tq), c, None), init)
    causal = jnp.where(lax.broadcasted_iota(jnp.int32, (tq, tq), 1) <= lax.broadcasted_iota(jnp.int32, (tq, tq), 0),
                       0.0, NEG)
    carry = tile(pl.multiple_of(i * tq, tq), carry, causal)
    for c in range(heads):
        o_ref[:, c * Dv:(c + 1) * Dv] = carry[c][2] / carry[c][1]


def mla_attention(q, k, v):
    B, H, S, W = q.shape
    tq = min(TQ_MLA, S)
    nq = S // tq
    Dv = v.shape[-1]
    hp = MLA_HEADS_PER_STEP
    vmem = hp * (2 * (S * W * 2 + S * Dv * 2) + 4 * tq * W * 2 + 8 * tq * tq * 4)
    return pl.pallas_call(
        functools.partial(_mla_attn_kernel, tq=tq, heads=hp),
        grid=(B, H // hp, nq),
        in_specs=[pl.BlockSpec((1, hp, tq, W), lambda b, h, i: (b, h, i, 0)),
                  pl.BlockSpec((1, hp, S, W), lambda b, h, i: (b, h, 0, 0), pipeline_mode=pl.Buffered(1)),
                  pl.BlockSpec((1, hp, S, Dv), lambda b, h, i: (b, h, 0, 0), pipeline_mode=pl.Buffered(1))],
        out_specs=pl.BlockSpec((tq, hp * Dv), lambda b, h, i: (b * nq + i, h)),
        out_shape=jax.ShapeDtypeStruct((B * S, H * Dv), F32),
        compiler_params=_cparams(("parallel", "parallel", "arbitrary"), vmem),
        name="mla_attention",
    )(q, k, v)


def _slab_lanes(ref, c, rows, nchunk, row0=0):
    return ref[pl.ds(row0 * nchunk + c, rows, stride=nchunk), :]


def _slab_row_dma(idx_ref, base, src_hbm, dst_ref, dst_row0, sem, nchunk):
    def copy(r):
        src = src_hbm.at[pl.ds(pl.multiple_of(idx_ref[base + r] * nchunk, nchunk), nchunk)]
        dst = dst_ref.at[pl.ds(pl.multiple_of((dst_row0 + r) * nchunk, nchunk), nchunk)]
        return pltpu.make_async_copy(src, dst, sem)

    def start(r, c):
        copy(r).start()
        return c

    def wait(r, c):
        copy(r).wait()
        return c

    return start, wait


def _out_proj_kernel(x_ref, a_ref, b_ref, ga_ref, gb_ref, wa_ref, wb_ref, o_ref, *rest, slab_out):
    na_ref, nb_ref = rest[-2:]
    j = pl.program_id(1)

    @pl.when(j == 0)
    def _():
        na_ref[...] = _rms(a_ref[...], ga_ref[...]).astype(BF16)
        nb_ref[...] = _rms(b_ref[...], gb_ref[...]).astype(BF16)

    o = (x_ref[...] + jnp.dot(na_ref[...], wa_ref[...], preferred_element_type=F32)
         + jnp.dot(nb_ref[...], wb_ref[...], preferred_element_type=F32))
    o_ref[...] = o
    if slab_out:
        s_ref = rest[0]
        tm, tn = o.shape
        per = tn // LANES
        nchunk = s_ref.shape[0] // tm
        for jj in range(nchunk // per):
            @pl.when(j == jj)
            def _():
                for c in range(per):
                    s_ref[pl.ds(jj * per + c, tm, stride=nchunk), :] = o[:, c * LANES:(c + 1) * LANES]


def out_proj(x, o_nsa, o_mla, g_nsa, g_mla, w_out, slab_out):
    T, D = x.shape
    Ka, Kb = o_nsa.shape[1], o_mla.shape[1]
    assert Ka == Kb
    tm, tn = min(TM, T), TN_OUT
    nchunk = D // LANES
    vmem = 2 * (tm * tn * 8 + 2 * tm * Ka * 4 + 2 * Ka * tn * 2) + 2 * tm * Ka * 2
    out_specs = [pl.BlockSpec((tm, tn), lambda i, j: (i, j))]
    out_shape = [jax.ShapeDtypeStruct((T, D), F32)]
    if slab_out:
        out_specs.append(pl.BlockSpec((tm * nchunk, LANES), lambda i, j: (i, 0)))
        out_shape.append(jax.ShapeDtypeStruct((T * nchunk, LANES), F32))
        vmem += 2 * tm * D * 4
    return pl.pallas_call(
        functools.partial(_out_proj_kernel, slab_out=slab_out),
        grid=(T // tm, D // tn),
        in_specs=[pl.BlockSpec((tm, tn), lambda i, j: (i, j)),
                  pl.BlockSpec((tm, Ka), lambda i, j: (i, 0)),
                  pl.BlockSpec((tm, Kb), lambda i, j: (i, 0)),
                  pl.BlockSpec((1, Ka), lambda i, j: (0, 0)),
                  pl.BlockSpec((1, Kb), lambda i, j: (0, 0)),
                  pl.BlockSpec((Ka, tn), lambda i, j: (0, j)),
                  pl.BlockSpec((Kb, tn), lambda i, j: (1, j))],
        out_specs=out_specs,
        out_shape=out_shape,
        scratch_shapes=[pltpu.VMEM((tm, Ka), BF16), pltpu.VMEM((tm, Kb), BF16)],
        compiler_params=_cparams(("parallel", "arbitrary"), 2 * vmem),
        name="out_proj",
    )(x, o_nsa, o_mla, g_nsa, g_mla, w_out, w_out)


def _swiglu_kernel(*refs, gathered):
    if gathered:
        be_ref, tok_ref, x_hbm, g_ref, wg_ref, wu_ref, wd_ref, o_ref, h_ref, acc_ref, slab_ref, sem = refs
    else:
        be_ref, x_ref, g_ref, wg_ref, wu_ref, wd_ref, o_ref, h_ref, acc_ref = refs
    tm, D = h_ref.shape
    nchunk = D // LANES
    i = pl.program_id(0)
    f = pl.program_id(1)
    live = i < be_ref[0]

    if gathered:
        def row_dma(block, slot):
            return _slab_row_dma(tok_ref, block * tm, x_hbm, slab_ref.at[slot], 0, sem.at[slot], nchunk)

        @pl.when((i == 0) & (f == 0))
        def _():
            lax.fori_loop(0, tm, row_dma(0, 0)[0], 0, unroll=GATHER_UNROLL)

        for slot in range(2):
            @pl.when(live & (f == 0) & (i % 2 == slot))
            def _():
                lax.fori_loop(0, tm, row_dma(i, slot)[1], 0, unroll=GATHER_UNROLL)
                rows = slab_ref.at[slot]
                ss = jnp.zeros((tm, 1), F32)
                for c in range(nchunk):
                    xc = _slab_lanes(rows, c, tm, nchunk)
                    ss = ss + jnp.sum(xc * xc, axis=-1, keepdims=True)
                r = lax.rsqrt(ss * (1.0 / D) + NORM_EPS)
                for c in range(nchunk):
                    cs = slice(c * LANES, (c + 1) * LANES)
                    h_ref[:, cs] = (_slab_lanes(rows, c, tm, nchunk) * r * g_ref[:, cs]).astype(BF16)
                acc_ref[...] = jnp.zeros_like(acc_ref)

            @pl.when((f == 1) & (i + 1 < be_ref[0]) & ((i + 1) % 2 == slot))
            def _():
                lax.fori_loop(0, tm, row_dma(i + 1, slot)[0], 0, unroll=GATHER_UNROLL)
    else:
        @pl.when(live & (f == 0))
        def _():
            h_ref[...] = _rms(x_ref[...], g_ref[...]).astype(BF16)
            acc_ref[...] = jnp.zeros_like(acc_ref)

    @pl.when(live)
    def _():
        h = h_ref[...]
        a = jax.nn.silu(jnp.dot(h, wg_ref[0], preferred_element_type=F32)) * jnp.dot(
            h, wu_ref[0], preferred_element_type=F32)
        acc_ref[...] += jnp.dot(a.astype(BF16), wd_ref[0], preferred_element_type=F32)

    last = f == pl.num_programs(1) - 1

    @pl.when(live & last)
    def _():
        if gathered:
            for c in range(nchunk):
                o_ref[pl.ds(c, tm, stride=nchunk), :] = acc_ref[:, c * LANES:(c + 1) * LANES]
        else:
            o_ref[...] = x_ref[...] + acc_ref[...]

    if gathered:
        @pl.when(jnp.logical_not(live) & last)
        def _():
            o_ref[...] = jnp.zeros_like(o_ref)


def swiglu_blocks(x, g, wg, wu, wd, block_info, tm, row_tok=None):
    gathered = row_tok is not None
    D = wg.shape[1]
    F = wg.shape[2]
    nchunk = D // LANES
    R = row_tok.shape[0] if gathered else x.shape[0]
    tf = TF
    nf = F // tf
    nsp = 2 if gathered else 1

    def wcol(i, f, be, *_):
        live = i < be[0]
        return (be[1 + i], 0, jnp.where(live, f, nf - 1))

    def wrow(i, f, be, *_):
        live = i < be[0]
        return (be[1 + i], jnp.where(live, f, nf - 1), 0)

    w_specs = [pl.BlockSpec((1, D), lambda i, f, *_: (0, 0)),
               pl.BlockSpec((1, D, tf), wcol),
               pl.BlockSpec((1, D, tf), wcol),
               pl.BlockSpec((1, tf, D), wrow)]
    scratch = [pltpu.VMEM((tm, D), BF16), pltpu.VMEM((tm, D), F32)]
    vmem = 4 * tm * D * 4 + tm * D * 2 + tm * D * 4 + 2 * 3 * D * tf * 2 + 6 * tm * tf * 4
    if gathered:
        in_specs = [pl.BlockSpec(memory_space=pl.ANY)] + w_specs
        out_spec = pl.BlockSpec((tm * nchunk, LANES), lambda i, f, *_: (i, 0))
        out_shape = jax.ShapeDtypeStruct((R * nchunk, LANES), F32)
        scratch += [pltpu.VMEM((2, tm * nchunk, LANES), F32), pltpu.SemaphoreType.DMA((2,))]
        prefetch = (block_info, row_tok)
    else:
        in_specs = [pl.BlockSpec((tm, D), lambda i, f, *_: (i, 0))] + w_specs
        out_spec = pl.BlockSpec((tm, D), lambda i, f, *_: (i, 0))
        out_shape = jax.ShapeDtypeStruct((R, D), F32)
        prefetch = (block_info,)
    grid_spec = pltpu.PrefetchScalarGridSpec(
        num_scalar_prefetch=nsp,
        grid=(R // tm, nf),
        in_specs=in_specs,
        out_specs=out_spec,
        scratch_shapes=scratch,
    )
    return pl.pallas_call(
        functools.partial(_swiglu_kernel, gathered=gathered),
        grid_spec=grid_spec,
        out_shape=out_shape,
        compiler_params=_cparams(("arbitrary" if gathered else "parallel", "arbitrary"), vmem + (4 << 20)),
        name="swiglu",
    )(*prefetch, x, g, wg, wu, wd)


def _router_kernel(x_ref, g_ref, w_ref, o_ref):
    h = _rms(x_ref[...], g_ref[...]).astype(BF16)
    logits = jnp.dot(h, w_ref[...], preferred_element_type=F32)
    lane = lax.broadcasted_iota(jnp.int32, logits.shape, 1)
    lf = lane.astype(F32)
    lg = jnp.where(lane < N_EXPERTS, logits, -jnp.inf)
    m1 = jnp.max(lg, axis=-1, keepdims=True)
    i1 = jnp.min(jnp.where(lg == m1, lf, float(LANES)), axis=-1, keepdims=True)
    lg2 = jnp.where(lf == i1, -jnp.inf, lg)
    m2 = jnp.max(lg2, axis=-1, keepdims=True)
    i2 = jnp.min(jnp.where(lg2 == m2, lf, float(LANES)), axis=-1, keepdims=True)
    e2 = jnp.exp(m2 - m1)
    den = 1.0 + e2
    o_ref[...] = jnp.where(lane == 0, i1, jnp.where(lane == 1, i2, jnp.where(lane == 2, 1.0 / den, e2 / den)))


def moe_router(x, g, w_router_p):
    T, D = x.shape
    tm = min(TM, T)
    return pl.pallas_call(
        _router_kernel,
        grid=(T // tm,),
        in_specs=[pl.BlockSpec((tm, D), lambda i: (i, 0)),
                  pl.BlockSpec((1, D), lambda i: (0, 0)),
                  pl.BlockSpec((D, LANES), lambda i: (0, 0))],
        out_specs=pl.BlockSpec((tm, LANES), lambda i: (i, 0)),
        out_shape=jax.ShapeDtypeStruct((T, LANES), F32),
        compiler_params=_cparams(("parallel",), 4 * tm * D * 4),
        name="moe_router",
    )(x, g, w_router_p)


def _combine_kernel(idx_ref, x_ref, y_hbm, gate_ref, o_ref, slab_ref, sem):
    tm, D = x_ref.shape
    nchunk = D // LANES
    i = pl.program_id(0)
    n_tok = idx_ref.shape[0] // TOP_K

    def row_dma(step, slot, which):
        for k in range(TOP_K):
            fn = _slab_row_dma(idx_ref, k * n_tok + step * tm, y_hbm, slab_ref.at[slot], k * tm, sem.at[slot],
                               nchunk)[which]
            lax.fori_loop(0, tm, fn, 0, unroll=GATHER_UNROLL)

    @pl.when(i == 0)
    def _():
        row_dma(0, 0, 0)

    gate = gate_ref[...]
    for slot in range(2):
        @pl.when((i + 1 < pl.num_programs(0)) & ((i + 1) % 2 == slot))
        def _():
            row_dma(i + 1, slot, 0)

    for slot in range(2):
        @pl.when(i % 2 == slot)
        def _():
            row_dma(i, slot, 1)
            rows = slab_ref.at[slot]
            for c in range(nchunk):
                cs = slice(c * LANES, (c + 1) * LANES)
                o_ref[:, cs] = x_ref[:, cs] + (_slab_lanes(rows, c, tm, nchunk) * gate[:, 2:3]
                                               + _slab_lanes(rows, c, tm, nchunk, row0=tm) * gate[:, 3:4])


def moe_combine(x, y_slab, pair_idx, route):
    T, D = x.shape
    tm = min(COMBINE_TM, T)
    nchunk = D // LANES
    grid_spec = pltpu.PrefetchScalarGridSpec(
        num_scalar_prefetch=1,
        grid=(T // tm,),
        in_specs=[pl.BlockSpec((tm, D), lambda i, idx: (i, 0)),
                  pl.BlockSpec(memory_space=pl.ANY),
                  pl.BlockSpec((tm, LANES), lambda i, idx: (i, 0))],
        out_specs=pl.BlockSpec((tm, D), lambda i, idx: (i, 0)),
        scratch_shapes=[pltpu.VMEM((2, TOP_K * tm * nchunk, LANES), F32), pltpu.SemaphoreType.DMA((2,))],
    )
    return pl.pallas_call(
        _combine_kernel,
        grid_spec=grid_spec,
        out_shape=jax.ShapeDtypeStruct((T, D), F32),
        compiler_params=_cparams(("arbitrary",), 10 * tm * D * 4),
        name="moe_combine",
    )(pair_idx, x, y_slab, route)


def moe_ffn(x, x_slab, g, w_router_p, wg, wu, wd, expert_base):
    T, D = x.shape
    n_assign = T * TOP_K
    route = moe_router(x, g, w_router_p)
    flat_e = route[:, :TOP_K].astype(jnp.int32).reshape(-1)
    order = jnp.argsort(flat_e)
    se = flat_e[order]
    stok = (order // TOP_K).astype(jnp.int32)
    counts = jnp.bincount(flat_e, length=N_EXPERTS)
    padded = (counts + MOE_TM - 1) // MOE_TM * MOE_TM
    pad_end = jnp.cumsum(padded)
    pad_start = pad_end - padded
    start = jnp.cumsum(counts) - counts
    dest = (pad_start[se] + jnp.arange(n_assign) - start[se]).astype(jnp.int32)
    n_blocks = -(-(n_assign + N_EXPERTS * (MOE_TM - 1)) // MOE_TM)
    n_rows = n_blocks * MOE_TM
    block_e = jnp.minimum(jnp.searchsorted(pad_end, jnp.arange(n_blocks) * MOE_TM, side='right'), N_EXPERTS - 1)
    n_live = (pad_end[-1] // MOE_TM).astype(jnp.int32)
    block_info = jnp.concatenate([n_live[None], block_e.astype(jnp.int32) + expert_base])
    row_e = jnp.repeat(block_e, MOE_TM)
    row_j = jnp.arange(n_rows) - pad_start[row_e]
    row_src = jnp.clip(start[row_e] + row_j, 0, n_assign - 1)
    row_tok = jnp.where(row_j < counts[row_e], stok[row_src], 0).astype(jnp.int32)
    dest_flat = dest[jnp.argsort(order)]
    pair_idx = jnp.concatenate([dest_flat[0::2], dest_flat[1::2]])

    y_slab = swiglu_blocks(x_slab, g, wg, wu, wd, block_info, MOE_TM, row_tok=row_tok)
    return moe_combine(x, y_slab, pair_idx, route)


def _rope_tab(pos, rot_dim):
    half = rot_dim // 2
    inv_freq = jnp.power(jnp.float32(ROPE_THETA), -jnp.arange(0, rot_dim, 2, dtype=F32) / rot_dim)
    ang = pos.astype(F32)[..., None] * inv_freq
    cos, sin = jnp.cos(ang), jnp.sin(ang)
    rest = LANES - rot_dim
    one = jnp.ones(pos.shape + (rest,), F32)
    zr = jnp.zeros(pos.shape + (rest,), F32)
    zh = jnp.zeros_like(sin)
    return jnp.stack([jnp.concatenate([cos, cos, one], -1),
                      jnp.concatenate([zh, sin, zr], -1),
                      jnp.concatenate([-sin, zh, zr], -1)])


def _pad_cols(w, n):
    return jnp.pad(w, ((0, 0), (0, n - w.shape[1])))


def kernel(x, positions, attn_norm, w_in, nsa_q_norm, nsa_k_norm, cmp_k_w1, cmp_k_w2, cmp_k_pos, cmp_v_w1, cmp_v_w2, cmp_v_pos, mla_q_a_norm, mla_w_q_up, mla_kv_a_norm, mla_w_kv_up, mla_q_norm, mla_k_norm, nsa_out_norm, mla_out_norm, w_out, ffn_norm, dense_w_gate, dense_w_up, dense_w_down, moe_router, moe_w_gate, moe_w_up, moe_w_down):
    B, S, D = x.shape
    T = B * S
    depth = w_in.shape[0]
    Dh = NSA_HEAD_DIM
    ncp = S // CMP_STRIDE
    kw = CMP_STRIDE * Dh

    rope_n = _rope_tab(positions, NSA_ROT_DIM).reshape(3, T, LANES)
    rope_m = _rope_tab(positions, MLA_ROPE_DIM).reshape(3, T, LANES)
    cmp_idx = jnp.minimum(jnp.arange(ncp) * CMP_STRIDE + CMP_BLOCK - 1, S - 1)
    rope_c = _rope_tab(positions[:, cmp_idx], NSA_ROT_DIM)

    dense_g, dense_u, dense_d = (w.astype(BF16) for w in (dense_w_gate, dense_w_up, dense_w_down))
    moe_g, moe_u, moe_d = (w.reshape((-1,) + w.shape[2:]).astype(BF16) for w in (moe_w_gate, moe_w_up, moe_w_down))

    g0 = NSA_Q_COLS + NSA_KV_COLS
    m0 = g0 + NSA_GATE_COLS
    xt = x.reshape(T, D)
    for layer in range(depth):
        w = w_in[layer]
        w_in_p = jnp.concatenate([w[:, :g0], w[:, m0:m0 + MLA_Q_RANK], w[:, m0 + MLA_Q_RANK:m0 + MLA_Q_RANK + MLA_KV_RANK],
                                  _pad_cols(w[:, m0 + MLA_Q_RANK + MLA_KV_RANK:], LANES),
                                  _pad_cols(w[:, g0:m0], LANES)], axis=1).astype(BF16)
        proj = rms_matmul(xt, attn_norm[layer][None], w_in_p, min(TM, T), TN_PROJ)

        gk = nsa_k_norm[layer]
        q_n, kc_raw, vc_raw, ks, vs, kwin, vwin = nsa_prep(proj, rope_n, nsa_q_norm[layer][None], gk, B, S)
        kc, vc = nsa_compress(
            kc_raw, vc_raw,
            cmp_k_w1[layer].reshape(2, kw, Dh).astype(BF16), cmp_k_w2[layer].astype(BF16),
            cmp_k_pos[layer].reshape(2, 1, kw),
            cmp_v_w1[layer].reshape(2, kw, Dh).astype(BF16), cmp_v_w2[layer].astype(BF16),
            cmp_v_pos[layer].reshape(2, 1, kw), rope_c, gk)
        o_nsa = nsa_attention(q_n, proj, kc, vc, ks, vs, kwin, vwin, B, S)

        wq = jnp.pad(mla_w_q_up[layer].reshape(MLA_Q_RANK, MLA_HEADS, MLA_QK_DIM),
                     ((0, 0), (0, 0), (0, MLA_QK_PAD - MLA_QK_DIM))).reshape(MLA_Q_RANK, MLA_HEADS * MLA_QK_PAD)
        gq = jnp.pad(mla_q_norm[layer], (0, MLA_QK_PAD - MLA_QK_DIM))[None]
        gkm = jnp.pad(mla_k_norm[layer], (0, MLA_QK_PAD - MLA_QK_DIM))[None]
        q_m = mla_q_prep(proj, mla_q_a_norm[layer][None], wq.astype(BF16), gq, rope_m, B, S)
        k_m, v_m = mla_kv_prep(proj, mla_kv_a_norm[layer][None], mla_w_kv_up[layer].astype(BF16), gkm, rope_m, B, S)
        o_mla = mla_attention(q_m, k_m, v_m)

        is_moe = layer % 2 == 1
        mixed = out_proj(xt, o_nsa, o_mla, nsa_out_norm[layer][None], mla_out_norm[layer][None],
                         w_out[layer].astype(BF16), slab_out=is_moe)

        i = layer // 2
        gf = ffn_norm[layer][None]
        if is_moe:
            xt, xt_slab = mixed
            xt = moe_ffn(xt, xt_slab, gf, _pad_cols(moe_router[i], LANES).astype(BF16), moe_g, moe_u, moe_d,
                         i * N_EXPERTS)
        else:
            xt, = mixed
            tm = min(TM, T)
            info = jnp.concatenate([jnp.full((1,), T // tm, jnp.int32), jnp.full((T // tm,), i, jnp.int32)])
            xt = swiglu_blocks(xt, gf, dense_g, dense_u, dense_d, info, tm)
    return xt.reshape(B, S, D)
```

```python
import functools

import jax
import jax.numpy as jnp
from jax import lax
from jax.experimental import pallas as pl
from jax.experimental.pallas import tpu as pltpu

F32 = jnp.float32
BF16 = jnp.bfloat16

LANES = 128
VMEM_CAP_BYTES = 56 * 1024 * 1024

LOG2E = 1.4426950408889634
NEG = -1e30

NORM_EPS = 1e-6
ROPE_THETA = 500000.0
Q_BLOCK = 128

NSA_HEAD_DIM = 128
NSA_KV_GROUPS = 2
NSA_HEADS_PER_GROUP = 4
NSA_HEADS = NSA_KV_GROUPS * NSA_HEADS_PER_GROUP
NSA_ROT_DIM = NSA_HEAD_DIM // 4
N_NSA_BRANCHES = 3
CMP_BLOCK = 32
CMP_STRIDE = 16
SEL_BLOCK = 64
SEL_SHIFT = SEL_BLOCK.bit_length() - 1
assert 1 << SEL_SHIFT == SEL_BLOCK
N_SEL_BLOCKS = 16
WINDOW = 512
SEL_FORCE = 1.0e4

MLA_V_DIM = 128
MLA_HEADS = 8
MLA_Q_RANK = 512
MLA_KV_RANK = 512
MLA_NOPE_DIM = 128
MLA_ROPE_DIM = 64
MLA_QK_DIM = MLA_NOPE_DIM + MLA_ROPE_DIM
MLA_QK_PAD = 256

N_EXPERTS = 8
TOP_K = 2

NSA_Q_COLS = NSA_HEADS * NSA_HEAD_DIM
NSA_KV_COLS = N_NSA_BRANCHES * 2 * NSA_KV_GROUPS * NSA_HEAD_DIM
NSA_GATE_COLS = NSA_HEADS * N_NSA_BRANCHES
P_Q0 = 0
P_KV0 = P_Q0 + NSA_Q_COLS
P_QA0 = P_KV0 + NSA_KV_COLS
P_KVA0 = P_QA0 + MLA_Q_RANK
P_KR0 = P_KVA0 + MLA_KV_RANK
P_G0 = P_KR0 + LANES
P_COLS = P_G0 + LANES

TM = 512
TN_PROJ = 1280
TN_OUT = 1024
TF = 512
TS_PREP = 1024
TK_ATT = 1024
TQ_MLA = 512
MLA_HEADS_PER_STEP = 4
MOE_TM = 512
COMBINE_TM = 256
GATHER_UNROLL = 8


def _cparams(sem, vmem_bytes):
    return pltpu.CompilerParams(dimension_semantics=sem,
                                vmem_limit_bytes=int(min(VMEM_CAP_BYTES, max(vmem_bytes, 16 * 1024 * 1024))))


def _rms(x, g):
    ms = jnp.mean(x * x, axis=-1, keepdims=True)
    return x * lax.rsqrt(ms + NORM_EPS) * g


def _rope_lanes(x, tab, half):
    return x * tab[0] + pltpu.roll(x, half, 1) * tab[1] + pltpu.roll(x, LANES - half, 1) * tab[2]


def _dot_nt(a, b):
    return lax.dot_general(a, b, (((1,), (1,)), ((), ())), preferred_element_type=F32)


def _rms_matmul_kernel(x_ref, g_ref, w_ref, o_ref, h_ref):
    @pl.when(pl.program_id(1) == 0)
    def _():
        h_ref[...] = _rms(x_ref[...], g_ref[...]).astype(BF16)

    o_ref[...] = jnp.dot(h_ref[...], w_ref[...], preferred_element_type=F32)


def rms_matmul(x, g, w, tm, tn):
    T, K = x.shape
    N = w.shape[1]
    vmem = 2 * tm * K * 4 + tm * K * 2 + 2 * K * tn * 2 + 2 * tm * tn * 4
    return pl.pallas_call(
        _rms_matmul_kernel,
        grid=(T // tm, N // tn),
        in_specs=[pl.BlockSpec((tm, K), lambda i, j: (i, 0)),
                  pl.BlockSpec((1, K), lambda i, j: (0, 0)),
                  pl.BlockSpec((K, tn), lambda i, j: (0, j))],
        out_specs=pl.BlockSpec((tm, tn), lambda i, j: (i, j)),
        out_shape=jax.ShapeDtypeStruct((T, N), F32),
        scratch_shapes=[pltpu.VMEM((tm, K), BF16)],
        compiler_params=_cparams(("parallel", "arbitrary"), 2 * vmem),
        name="rms_matmul",
    )(x, g, w)


def _nsa_prep_kernel(q_in, kc_in, vc_in, ks_in, vs_in, kw_in, vw_in, rope_ref, gq_ref, gk_ref,
                     q_out, kc_out, vc_out, ks_out, vs_out, kw_out, vw_out):
    tab = rope_ref[...]
    half = NSA_ROT_DIM // 2
    Dh = NSA_HEAD_DIM
    fold = Dh ** -0.5 * LOG2E
    for h in range(NSA_HEADS_PER_GROUP):
        qh = _rope_lanes(_rms(q_in[:, h * Dh:(h + 1) * Dh], gq_ref[...]), tab, half)
        q_out[0, 0, h] = (qh * fold).astype(BF16)
    kc_out[0, 0] = kc_in[...]
    vc_out[0, 0] = vc_in[...]
    ks_out[0, 0] = _rope_lanes(_rms(ks_in[...], gk_ref[1:2, :]), tab, half).astype(BF16)
    kw_out[0, 0] = _rope_lanes(_rms(kw_in[...], gk_ref[2:3, :]), tab, half).astype(BF16)
    vs_out[0, 0] = vs_in[...].astype(BF16)
    vw_out[0, 0] = vw_in[...].astype(BF16)


def nsa_prep(proj, rope_n, gq, gk, B, S):
    ts = min(TS_PREP, S)
    ns = S // ts
    G, Hg, Dh = NSA_KV_GROUPS, NSA_HEADS_PER_GROUP, NSA_HEAD_DIM
    kv_blk0 = P_KV0 // Dh

    def in_spec(branch, kv):
        off = kv_blk0 + (branch * 2 + kv) * G
        return pl.BlockSpec((ts, Dh), lambda b, s, g: (b * ns + s, off + g))

    out_spec = pl.BlockSpec((1, 1, ts, Dh), lambda b, s, g: (b, g, s, 0))
    f32_out = jax.ShapeDtypeStruct((B, G, S, Dh), F32)
    bf_out = jax.ShapeDtypeStruct((B, G, S, Dh), BF16)
    return pl.pallas_call(
        _nsa_prep_kernel,
        grid=(B, ns, G),
        in_specs=[pl.BlockSpec((ts, Hg * Dh), lambda b, s, g: (b * ns + s, P_Q0 // (Hg * Dh) + g)),
                  in_spec(0, 0), in_spec(0, 1), in_spec(1, 0), in_spec(1, 1), in_spec(2, 0), in_spec(2, 1),
                  pl.BlockSpec((3, ts, Dh), lambda b, s, g: (0, b * ns + s, 0)),
                  pl.BlockSpec((1, Dh), lambda b, s, g: (0, 0)),
                  pl.BlockSpec((N_NSA_BRANCHES, Dh), lambda b, s, g: (0, 0))],
        out_specs=[pl.BlockSpec((1, 1, Hg, ts, Dh), lambda b, s, g: (b, g, 0, s, 0))] + [out_spec] * 6,
        out_shape=[jax.ShapeDtypeStruct((B, G, Hg, S, Dh), BF16), f32_out, f32_out, bf_out, bf_out, bf_out, bf_out],
        compiler_params=_cparams(("parallel", "parallel", "parallel"), 2 * 2 * 20 * ts * Dh * 4),
        name="nsa_prep",
    )(proj, proj, proj, proj, proj, proj, proj, rope_n, gq, gk)


def _compress_one(t_ref, w1_ref, w2_ref, pos_ref):
    t = t_ref[0, 0]
    ncp = t.shape[0]
    u = jnp.dot((t + pos_ref[0]).astype(BF16), w1_ref[0], preferred_element_type=F32)
    v = jnp.dot((t + pos_ref[1]).astype(BF16), w1_ref[1], preferred_element_type=F32)
    pre = u + pltpu.roll(v, ncp - 1, 0)
    return jnp.dot(jax.nn.gelu(pre).astype(BF16), w2_ref[...], preferred_element_type=F32)


def _compress_kernel(tk_ref, tv_ref, w1k_ref, w2k_ref, posk_ref, w1v_ref, w2v_ref, posv_ref, rope_ref, gk_ref,
                     kc_ref, vc_ref):
    k = _compress_one(tk_ref, w1k_ref, w2k_ref, posk_ref)
    kc_ref[0, 0] = _rope_lanes(_rms(k, gk_ref[0:1, :]), rope_ref[:, 0], NSA_ROT_DIM // 2).astype(BF16)
    vc_ref[0, 0] = _compress_one(tv_ref, w1v_ref, w2v_ref, posv_ref).astype(BF16)


def nsa_compress(kc_raw, vc_raw, w1k, w2k, posk, w1v, w2v, posv, rope_c, gk):
    B, G, S, Dh = kc_raw.shape
    ncp = S // CMP_STRIDE
    kw = CMP_STRIDE * Dh
    tk = kc_raw.reshape(B, G, ncp, kw)
    tv = vc_raw.reshape(B, G, ncp, kw)
    t_spec = pl.BlockSpec((1, 1, ncp, kw), lambda b, g: (b, g, 0, 0))
    w1_spec = pl.BlockSpec((2, kw, Dh), lambda b, g: (0, 0, 0))
    w2_spec = pl.BlockSpec((Dh, Dh), lambda b, g: (0, 0))
    pos_spec = pl.BlockSpec((2, 1, kw), lambda b, g: (0, 0, 0))
    out_spec = pl.BlockSpec((1, 1, ncp, Dh), lambda b, g: (b, g, 0, 0))
    out = jax.ShapeDtypeStruct((B, G, ncp, Dh), BF16)
    return pl.pallas_call(
        _compress_kernel,
        grid=(B, G),
        in_specs=[t_spec, t_spec, w1_spec, w2_spec, pos_spec, w1_spec, w2_spec, pos_spec,
                  pl.BlockSpec((3, 1, ncp, Dh), lambda b, g: (0, b, 0, 0)),
                  pl.BlockSpec((N_NSA_BRANCHES, Dh), lambda b, g: (0, 0))],
        out_specs=[out_spec, out_spec],
        out_shape=[out, out],
        compiler_params=_cparams(("parallel", "parallel"), 2 * (4 * ncp * kw * 4 + 8 * kw * Dh * 2)),
        name="nsa_compress",
    )(tk, tv, w1k, w2k, posk, w1v, w2v, posv, rope_c, gk)


def _online_softmax(s, m, l):
    m_new = jnp.maximum(m, jnp.max(s, axis=-1, keepdims=True))
    alpha = jnp.exp2(m - m_new)
    p = jnp.exp2(s - m_new)
    return p, m_new, alpha * l + jnp.sum(p, axis=-1, keepdims=True), alpha


def _softmax_rows(s):
    p = jnp.exp2(s - jnp.max(s, axis=-1, keepdims=True))
    return p, jnp.sum(p, axis=-1, keepdims=True)


def _nsa_attn_kernel(q_ref, gl_ref, kc_ref, vc_ref, ks_ref, vs_ref, kw_ref, vw_ref, o_ref,
                     *, seq, n_sel):
    G, Hg, Dh, QB = NSA_KV_GROUPS, NSA_HEADS_PER_GROUP, NSA_HEAD_DIM, Q_BLOCK
    R = Hg * QB
    s0 = pl.program_id(1) * QB
    ncp = kc_ref.shape[2]
    nb = max(LANES, seq // SEL_BLOCK)
    tk = TK_ATT
    wk = WINDOW + QB

    t_q = s0 + lax.broadcasted_iota(jnp.int32, (QB, 1), 0)
    t_l = s0 + lax.broadcasted_iota(jnp.int32, (1, QB), 1)

    cmp_end = lax.broadcasted_iota(jnp.int32, (1, ncp), 1) * CMP_STRIDE + (CMP_BLOCK - 1)
    bias_c = jnp.where(cmp_end <= t_q, 0.0, NEG)
    row_ok = jnp.where(t_q >= CMP_BLOCK - 1, 1.0, 0.0)
    w0 = pl.multiple_of(jnp.maximum(s0 - WINDOW, 0), QB)
    diff = t_q - (w0 + lax.broadcasted_iota(jnp.int32, (1, wk), 1))
    bias_w = jnp.where((diff >= 0) & (diff < WINDOW), 0.0, NEG)

    j_i = lax.broadcasted_iota(jnp.int32, (nb, ncp), 0) * SEL_BLOCK
    c_i = lax.broadcasted_iota(jnp.int32, (nb, ncp), 1) * CMP_STRIDE
    overlap_t = jnp.where((c_i < j_i + SEL_BLOCK) & (c_i + (CMP_BLOCK - 1) >= j_i), 1.0, 0.0).astype(BF16)
    jt = lax.broadcasted_iota(jnp.int32, (nb, QB), 0)
    cur = jnp.right_shift(t_l, SEL_SHIFT)
    force_t = jnp.where((jt == 0) | (jt == cur) | (jt == cur - 1), SEL_FORCE, 0.0)
    eligible_t = jt <= cur
    jf = jt.astype(F32)

    qbs, sels, o_cs, o_ws = [], [], [], []
    for g in range(G):
        qb = q_ref[0, g].reshape(R, Dh)
        qbs.append(qb)

        p, l = _softmax_rows(_dot_nt(qb, kc_ref[0, g]).reshape(Hg, QB, ncp) + bias_c[None])
        p = p * (row_ok[None] / l)
        o_cs.append(jnp.dot(p.reshape(R, ncp).astype(BF16), vc_ref[0, g], preferred_element_type=F32))

        imp = _dot_nt(overlap_t, jnp.sum(p, axis=0).astype(BF16))
        imp = jnp.where(eligible_t, imp + force_t, -jnp.inf)
        sel = jnp.zeros((nb, QB), F32)
        for _ in range(n_sel):
            mx = jnp.max(imp, axis=0, keepdims=True)
            first = jnp.min(jnp.where(imp == mx, jf, float(nb)), axis=0, keepdims=True)
            pick = (jf == first) & (mx > -jnp.inf)
            sel = jnp.where(pick, 1.0, sel)
            imp = jnp.where(pick, -jnp.inf, imp)
        sels.append(sel.T.astype(BF16))

        p, l = _softmax_rows(_dot_nt(qb, kw_ref[0, g, pl.ds(w0, wk), :]).reshape(Hg, QB, wk) + bias_w[None])
        o_w = jnp.dot(p.reshape(R, wk).astype(BF16), vw_ref[0, g, pl.ds(w0, wk), :], preferred_element_type=F32)
        o_ws.append(o_w / l.reshape(R, 1))

    jb = lax.broadcasted_iota(jnp.int32, (nb, tk), 0)
    kl = lax.broadcasted_iota(jnp.int32, (nb, tk), 1)
    kl_row = lax.broadcasted_iota(jnp.int32, (1, tk), 1)

    def sel_step(kt, carry):
        k0 = pl.multiple_of(kt * tk, tk)
        expand = jnp.where(jb == jnp.right_shift(k0 + kl, SEL_SHIFT), 1.0, 0.0).astype(BF16)
        causal = k0 + kl_row <= t_q
        out = []
        for g in range(G):
            m, l, acc = carry[g]
            chosen = jnp.dot(sels[g], expand, preferred_element_type=F32)
            bias = jnp.where((chosen > 0.5) & causal, 0.0, NEG)
            s = _dot_nt(qbs[g], ks_ref[0, g, pl.ds(k0, tk), :]).reshape(Hg, QB, tk) + bias[None]
            p, m, l, alpha = _online_softmax(s, m, l)
            pv = jnp.dot(p.reshape(R, tk).astype(BF16), vs_ref[0, g, pl.ds(k0, tk), :], preferred_element_type=F32)
            out.append((m, l, alpha * acc + pv.reshape(Hg, QB, Dh)))
        return tuple(out)

    init = tuple((jnp.full((Hg, QB, 1), NEG, F32), jnp.zeros((Hg, QB, 1), F32), jnp.zeros((Hg, QB, Dh), F32))
                 for _ in range(G))
    final = lax.fori_loop(0, (s0 + QB + tk - 1) // tk, sel_step, init)

    gates = jax.nn.sigmoid(gl_ref[...])
    for g in range(G):
        o_s = (final[g][2] / final[g][1]).reshape(R, Dh)
        for h in range(Hg):
            c0 = (g * Hg + h) * N_NSA_BRANCHES
            rs = slice(h * QB, (h + 1) * QB)
            o_ref[:, (g * Hg + h) * Dh:(g * Hg + h + 1) * Dh] = (
                gates[:, c0:c0 + 1] * o_cs[g][rs] + gates[:, c0 + 1:c0 + 2] * o_s[rs]
                + gates[:, c0 + 2:c0 + 3] * o_ws[g][rs])


def nsa_attention(q, proj, kc, vc, ks, vs, kw, vw, B, S):
    G, Hg, Dh, QB = NSA_KV_GROUPS, NSA_HEADS_PER_GROUP, NSA_HEAD_DIM, Q_BLOCK
    nq = S // QB
    ncp = kc.shape[2]
    n_sel = min(N_SEL_BLOCKS, S // SEL_BLOCK)
    qcols = G * Hg * Dh
    kv_spec = pl.BlockSpec((1, G, S, Dh), lambda b, i: (b, 0, 0, 0), pipeline_mode=pl.Buffered(1))
    c_spec = pl.BlockSpec((1, G, ncp, Dh), lambda b, i: (b, 0, 0, 0), pipeline_mode=pl.Buffered(1))
    sw = max(TK_ATT, ncp, WINDOW + QB)
    vmem = G * (4 * S * Dh * 2 + 2 * ncp * Dh * 2) + 10 * G * Hg * QB * sw * 4
    return pl.pallas_call(
        functools.partial(_nsa_attn_kernel, seq=S, n_sel=n_sel),
        grid=(B, nq),
        in_specs=[pl.BlockSpec((1, G, Hg, QB, Dh), lambda b, i: (b, 0, 0, i, 0)),
                  pl.BlockSpec((QB, LANES), lambda b, i: (b * nq + i, P_G0 // LANES)),
                  c_spec, c_spec, kv_spec, kv_spec, kv_spec, kv_spec],
        out_specs=pl.BlockSpec((QB, qcols), lambda b, i: (b * nq + i, 0)),
        out_shape=jax.ShapeDtypeStruct((B * S, qcols), F32),
        compiler_params=_cparams(("parallel", "arbitrary"), vmem),
        name="nsa_attention",
    )(q, proj, kc, vc, ks, vs, kw, vw)


def _mla_q_kernel(qa_ref, ga_ref, w_ref, gq_ref, rope_ref, o_ref):
    W = MLA_QK_PAD
    c = _rms(qa_ref[...], ga_ref[...]).astype(BF16)
    q = jnp.dot(c, w_ref[...], preferred_element_type=F32)
    tab = rope_ref[...]
    fold = MLA_QK_DIM ** -0.5 * LOG2E
    for h in range(MLA_HEADS):
        qh = q[:, h * W:(h + 1) * W]
        ms = jnp.sum(qh * qh, axis=-1, keepdims=True) * (1.0 / MLA_QK_DIM)
        qn = qh * lax.rsqrt(ms + NORM_EPS) * gq_ref[...]
        o_ref[0, h, :, :LANES] = (qn[:, :LANES] * fold).astype(BF16)
        o_ref[0, h, :, LANES:] = (_rope_lanes(qn[:, LANES:], tab, MLA_ROPE_DIM // 2) * fold).astype(BF16)


def mla_q_prep(proj, ga, wq, gq, rope_m, B, S):
    tm = min(TM, S)
    ns = S // tm
    H, R, W = MLA_HEADS, MLA_Q_RANK, MLA_QK_PAD
    return pl.pallas_call(
        _mla_q_kernel,
        grid=(B * ns,),
        in_specs=[pl.BlockSpec((tm, R), lambda r: (r, P_QA0 // R)),
                  pl.BlockSpec((1, R), lambda r: (0, 0)),
                  pl.BlockSpec((R, H * W), lambda r: (0, 0)),
                  pl.BlockSpec((1, W), lambda r: (0, 0)),
                  pl.BlockSpec((3, tm, LANES), lambda r: (0, r, 0))],
        out_specs=pl.BlockSpec((1, H, tm, W), lambda r: (r // ns, 0, r % ns, 0)),
        out_shape=jax.ShapeDtypeStruct((B, H, S, W), BF16),
        compiler_params=_cparams(("parallel",), 2 * (tm * R * 4 + R * H * W * 2 + tm * H * W * 2) + 3 * tm * H * W * 4),
        name="mla_q_prep",
    )(proj, ga, wq, gq, rope_m)


def _mla_kv_kernel(kva_ref, kr_ref, ga_ref, w_ref, gk_ref, rope_ref, k_ref, v_ref):
    W = MLA_QK_PAD
    c = _rms(kva_ref[...], ga_ref[...]).astype(BF16)
    kv = jnp.dot(c, w_ref[...], preferred_element_type=F32)
    tab = rope_ref[...]
    k_rot = kr_ref[...]
    ss_rot = jnp.sum(k_rot * k_rot, axis=-1, keepdims=True)
    for h in range(MLA_HEADS):
        k_nope = kv[:, h * W:h * W + LANES]
        ms = (jnp.sum(k_nope * k_nope, axis=-1, keepdims=True) + ss_rot) * (1.0 / MLA_QK_DIM)
        r = lax.rsqrt(ms + NORM_EPS)
        k_ref[0, h, :, :LANES] = (k_nope * r * gk_ref[:, :LANES]).astype(BF16)
        k_ref[0, h, :, LANES:] = _rope_lanes(k_rot * r * gk_ref[:, LANES:], tab, MLA_ROPE_DIM // 2).astype(BF16)
        v_ref[0, h] = kv[:, h * W + LANES:(h + 1) * W].astype(BF16)


def mla_kv_prep(proj, ga, wkv, gk, rope_m, B, S):
    tm = min(TM, S)
    ns = S // tm
    H, R, W = MLA_HEADS, MLA_KV_RANK, MLA_QK_PAD
    return pl.pallas_call(
        _mla_kv_kernel,
        grid=(B * ns,),
        in_specs=[pl.BlockSpec((tm, R), lambda r: (r, P_KVA0 // R)),
                  pl.BlockSpec((tm, LANES), lambda r: (r, P_KR0 // LANES)),
                  pl.BlockSpec((1, R), lambda r: (0, 0)),
                  pl.BlockSpec((R, H * W), lambda r: (0, 0)),
                  pl.BlockSpec((1, W), lambda r: (0, 0)),
                  pl.BlockSpec((3, tm, LANES), lambda r: (0, r, 0))],
        out_specs=[pl.BlockSpec((1, H, tm, W), lambda r: (r // ns, 0, r % ns, 0)),
                   pl.BlockSpec((1, H, tm, MLA_V_DIM), lambda r: (r // ns, 0, r % ns, 0))],
        out_shape=[jax.ShapeDtypeStruct((B, H, S, W), BF16), jax.ShapeDtypeStruct((B, H, S, MLA_V_DIM), BF16)],
        compiler_params=_cparams(("parallel",), 2 * (tm * R * 4 + R * H * W * 2 + tm * H * W * 3) + 3 * tm * H * W * 4),
        name="mla_kv_prep",
    )(proj, proj, ga, wkv, gk, rope_m)


def _mla_attn_kernel(q_ref, k_ref, v_ref, o_ref, *, tq, heads):
    i = pl.program_id(2)
    Dv = v_ref.shape[-1]

    def tile(k0, carry, bias):
        out = []
        for c in range(heads):
            m, l, acc = carry[c]
            s = _dot_nt(q_ref[0, c], k_ref[0, c, pl.ds(k0, tq), :])
            if bias is not None:
                s = s + bias
            p, m, l, alpha = _online_softmax(s, m, l)
            pv = jnp.dot(p.astype(BF16), v_ref[0, c, pl.ds(k0, tq), :], preferred_element_type=F32)
            out.append((m, l, alpha * acc + pv))
        return tuple(out)

    init = tuple((jnp.full((tq, 1), NEG, F32), jnp.zeros((tq, 1), F32), jnp.zeros((tq, Dv), F32))
                 for _ in range(heads))
    carry = lax.fori_loop(0, i, lambda kt, c: tile(pl.multiple_of(kt * tq, tq), c, None), init)
    causal = jnp.where(lax.broadcasted_iota(jnp.int32, (tq, tq), 1) <= lax.broadcasted_iota(jnp.int32, (tq, tq), 0),
                       0.0, NEG)
    carry = tile(pl.multiple_of(i * tq, tq), carry, causal)
    for c in range(heads):
        o_ref[:, c * Dv:(c + 1) * Dv] = carry[c][2] / carry[c][1]


def mla_attention(q, k, v):
    B, H, S, W = q.shape
    tq = min(TQ_MLA, S)
    nq = S // tq
    Dv = v.shape[-1]
    hp = MLA_HEADS_PER_STEP
    vmem = hp * (2 * (S * W * 2 + S * Dv * 2) + 4 * tq * W * 2 + 8 * tq * tq * 4)
    return pl.pallas_call(
        functools.partial(_mla_attn_kernel, tq=tq, heads=hp),
        grid=(B, H // hp, nq),
        in_specs=[pl.BlockSpec((1, hp, tq, W), lambda b, h, i: (b, h, i, 0)),
                  pl.BlockSpec((1, hp, S, W), lambda b, h, i: (b, h, 0, 0), pipeline_mode=pl.Buffered(1)),
                  pl.BlockSpec((1, hp, S, Dv), lambda b, h, i: (b, h, 0, 0), pipeline_mode=pl.Buffered(1))],
        out_specs=pl.BlockSpec((tq, hp * Dv), lambda b, h, i: (b * nq + i, h)),
        out_shape=jax.ShapeDtypeStruct((B * S, H * Dv), F32),
        compiler_params=_cparams(("parallel", "parallel", "arbitrary"), vmem),
        name="mla_attention",
    )(q, k, v)


def _slab_lanes(ref, c, rows, nchunk, row0=0):
    return ref[pl.ds(row0 * nchunk + c, rows, stride=nchunk), :]


def _slab_row_dma(idx_ref, base, src_hbm, dst_ref, dst_row0, sem, nchunk):
    def copy(r):
        src = src_hbm.at[pl.ds(pl.multiple_of(idx_ref[base + r] * nchunk, nchunk), nchunk)]
        dst = dst_ref.at[pl.ds(pl.multiple_of((dst_row0 + r) * nchunk, nchunk), nchunk)]
        return pltpu.make_async_copy(src, dst, sem)

    def start(r, c):
        copy(r).start()
        return c

    def wait(r, c):
        copy(r).wait()
        return c

    return start, wait


def _out_proj_kernel(x_ref, a_ref, b_ref, ga_ref, gb_ref, wa_ref, wb_ref, o_ref, *rest, slab_out):
    na_ref, nb_ref = rest[-2:]
    j = pl.program_id(1)

    @pl.when(j == 0)
    def _():
        na_ref[...] = _rms(a_ref[...], ga_ref[...]).astype(BF16)
        nb_ref[...] = _rms(b_ref[...], gb_ref[...]).astype(BF16)

    o = (x_ref[...] + jnp.dot(na_ref[...], wa_ref[...], preferred_element_type=F32)
         + jnp.dot(nb_ref[...], wb_ref[...], preferred_element_type=F32))
    o_ref[...] = o
    if slab_out:
        s_ref = rest[0]
        tm, tn = o.shape
        per = tn // LANES
        nchunk = s_ref.shape[0] // tm
        for jj in range(nchunk // per):
            @pl.when(j == jj)
            def _():
                for c in range(per):
                    s_ref[pl.ds(jj * per + c, tm, stride=nchunk), :] = o[:, c * LANES:(c + 1) * LANES]


def out_proj(x, o_nsa, o_mla, g_nsa, g_mla, w_out, slab_out):
    T, D = x.shape
    Ka, Kb = o_nsa.shape[1], o_mla.shape[1]
    assert Ka == Kb
    tm, tn = min(TM, T), TN_OUT
    nchunk = D // LANES
    vmem = 2 * (tm * tn * 8 + 2 * tm * Ka * 4 + 2 * Ka * tn * 2) + 2 * tm * Ka * 2
    out_specs = [pl.BlockSpec((tm, tn), lambda i, j: (i, j))]
    out_shape = [jax.ShapeDtypeStruct((T, D), F32)]
    if slab_out:
        out_specs.append(pl.BlockSpec((tm * nchunk, LANES), lambda i, j: (i, 0)))
        out_shape.append(jax.ShapeDtypeStruct((T * nchunk, LANES), F32))
        vmem += 2 * tm * D * 4
    return pl.pallas_call(
        functools.partial(_out_proj_kernel, slab_out=slab_out),
        grid=(T // tm, D // tn),
        in_specs=[pl.BlockSpec((tm, tn), lambda i, j: (i, j)),
                  pl.BlockSpec((tm, Ka), lambda i, j: (i, 0)),
                  pl.BlockSpec((tm, Kb), lambda i, j: (i, 0)),
                  pl.BlockSpec((1, Ka), lambda i, j: (0, 0)),
                  pl.BlockSpec((1, Kb), lambda i, j: (0, 0)),
                  pl.BlockSpec((Ka, tn), lambda i, j: (0, j)),
                  pl.BlockSpec((Kb, tn), lambda i, j: (1, j))],
        out_specs=out_specs,
        out_shape=out_shape,
        scratch_shapes=[pltpu.VMEM((tm, Ka), BF16), pltpu.VMEM((tm, Kb), BF16)],
        compiler_params=_cparams(("parallel", "arbitrary"), 2 * vmem),
        name="out_proj",
    )(x, o_nsa, o_mla, g_nsa, g_mla, w_out, w_out)


def _swiglu_kernel(*refs, gathered):
    if gathered:
        be_ref, tok_ref, x_hbm, g_ref, wg_ref, wu_ref, wd_ref, o_ref, h_ref, acc_ref, slab_ref, sem = refs
    else:
        be_ref, x_ref, g_ref, wg_ref, wu_ref, wd_ref, o_ref, h_ref, acc_ref = refs
    tm, D = h_ref.shape
    nchunk = D // LANES
    i = pl.program_id(0)
    f = pl.program_id(1)
    live = i < be_ref[0]

    if gathered:
        def row_dma(block, slot):
            return _slab_row_dma(tok_ref, block * tm, x_hbm, slab_ref.at[slot], 0, sem.at[slot], nchunk)

        @pl.when((i == 0) & (f == 0))
        def _():
            lax.fori_loop(0, tm, row_dma(0, 0)[0], 0, unroll=GATHER_UNROLL)

        for slot in range(2):
            @pl.when(live & (f == 0) & (i % 2 == slot))
            def _():
                lax.fori_loop(0, tm, row_dma(i, slot)[1], 0, unroll=GATHER_UNROLL)
                rows = slab_ref.at[slot]
                ss = jnp.zeros((tm, 1), F32)
                for c in range(nchunk):
                    xc = _slab_lanes(rows, c, tm, nchunk)
                    ss = ss + jnp.sum(xc * xc, axis=-1, keepdims=True)
                r = lax.rsqrt(ss * (1.0 / D) + NORM_EPS)
                for c in range(nchunk):
                    cs = slice(c * LANES, (c + 1) * LANES)
                    h_ref[:, cs] = (_slab_lanes(rows, c, tm, nchunk) * r * g_ref[:, cs]).astype(BF16)
                acc_ref[...] = jnp.zeros_like(acc_ref)

            @pl.when((f == 1) & (i + 1 < be_ref[0]) & ((i + 1) % 2 == slot))
            def _():
                lax.fori_loop(0, tm, row_dma(i + 1, slot)[0], 0, unroll=GATHER_UNROLL)
    else:
        @pl.when(live & (f == 0))
        def _():
            h_ref[...] = _rms(x_ref[...], g_ref[...]).astype(BF16)
            acc_ref[...] = jnp.zeros_like(acc_ref)

    @pl.when(live)
    def _():
        h = h_ref[...]
        a = jax.nn.silu(jnp.dot(h, wg_ref[0], preferred_element_type=F32)) * jnp.dot(
            h, wu_ref[0], preferred_element_type=F32)
        acc_ref[...] += jnp.dot(a.astype(BF16), wd_ref[0], preferred_element_type=F32)

    last = f == pl.num_programs(1) - 1

    @pl.when(live & last)
    def _():
        if gathered:
            for c in range(nchunk):
                o_ref[pl.ds(c, tm, stride=nchunk), :] = acc_ref[:, c * LANES:(c + 1) * LANES]
        else:
            o_ref[...] = x_ref[...] + acc_ref[...]

    if gathered:
        @pl.when(jnp.logical_not(live) & last)
        def _():
            o_ref[...] = jnp.zeros_like(o_ref)


def swiglu_blocks(x, g, wg, wu, wd, block_info, tm, row_tok=None):
    gathered = row_tok is not None
    D = wg.shape[1]
    F = wg.shape[2]
    nchunk = D // LANES
    R = row_tok.shape[0] if gathered else x.shape[0]
    tf = TF
    nf = F // tf
    nsp = 2 if gathered else 1

    def wcol(i, f, be, *_):
        live = i < be[0]
        return (be[1 + i], 0, jnp.where(live, f, nf - 1))

    def wrow(i, f, be, *_):
        live = i < be[0]
        return (be[1 + i], jnp.where(live, f, nf - 1), 0)

    w_specs = [pl.BlockSpec((1, D), lambda i, f, *_: (0, 0)),
               pl.BlockSpec((1, D, tf), wcol),
               pl.BlockSpec((1, D, tf), wcol),
               pl.BlockSpec((1, tf, D), wrow)]
    scratch = [pltpu.VMEM((tm, D), BF16), pltpu.VMEM((tm, D), F32)]
    vmem = 4 * tm * D * 4 + tm * D * 2 + tm * D * 4 + 2 * 3 * D * tf * 2 + 6 * tm * tf * 4
    if gathered:
        in_specs = [pl.BlockSpec(memory_space=pl.ANY)] + w_specs
        out_spec = pl.BlockSpec((tm * nchunk, LANES), lambda i, f, *_: (i, 0))
        out_shape = jax.ShapeDtypeStruct((R * nchunk, LANES), F32)
        scratch += [pltpu.VMEM((2, tm * nchunk, LANES), F32), pltpu.SemaphoreType.DMA((2,))]
        prefetch = (block_info, row_tok)
    else:
        in_specs = [pl.BlockSpec((tm, D), lambda i, f, *_: (i, 0))] + w_specs
        out_spec = pl.BlockSpec((tm, D), lambda i, f, *_: (i, 0))
        out_shape = jax.ShapeDtypeStruct((R, D), F32)
        prefetch = (block_info,)
    grid_spec = pltpu.PrefetchScalarGridSpec(
        num_scalar_prefetch=nsp,
        grid=(R // tm, nf),
        in_specs=in_specs,
        out_specs=out_spec,
        scratch_shapes=scratch,
    )
    return pl.pallas_call(
        functools.partial(_swiglu_kernel, gathered=gathered),
        grid_spec=grid_spec,
        out_shape=out_shape,
        compiler_params=_cparams(("arbitrary" if gathered else "parallel", "arbitrary"), vmem + (4 << 20)),
        name="swiglu",
    )(*prefetch, x, g, wg, wu, wd)


def _router_kernel(x_ref, g_ref, w_ref, o_ref):
    h = _rms(x_ref[...], g_ref[...]).astype(BF16)
    logits = jnp.dot(h, w_ref[...], preferred_element_type=F32)
    lane = lax.broadcasted_iota(jnp.int32, logits.shape, 1)
    lf = lane.astype(F32)
    lg = jnp.where(lane < N_EXPERTS, logits, -jnp.inf)
    m1 = jnp.max(lg, axis=-1, keepdims=True)
    i1 = jnp.min(jnp.where(lg == m1, lf, float(LANES)), axis=-1, keepdims=True)
    lg2 = jnp.where(lf == i1, -jnp.inf, lg)
    m2 = jnp.max(lg2, axis=-1, keepdims=True)
    i2 = jnp.min(jnp.where(lg2 == m2, lf, float(LANES)), axis=-1, keepdims=True)
    e2 = jnp.exp(m2 - m1)
    den = 1.0 + e2
    o_ref[...] = jnp.where(lane == 0, i1, jnp.where(lane == 1, i2, jnp.where(lane == 2, 1.0 / den, e2 / den)))


def moe_router(x, g, w_router_p):
    T, D = x.shape
    tm = min(TM, T)
    return pl.pallas_call(
        _router_kernel,
        grid=(T // tm,),
        in_specs=[pl.BlockSpec((tm, D), lambda i: (i, 0)),
                  pl.BlockSpec((1, D), lambda i: (0, 0)),
                  pl.BlockSpec((D, LANES), lambda i: (0, 0))],
        out_specs=pl.BlockSpec((tm, LANES), lambda i: (i, 0)),
        out_shape=jax.ShapeDtypeStruct((T, LANES), F32),
        compiler_params=_cparams(("parallel",), 4 * tm * D * 4),
        name="moe_router",
    )(x, g, w_router_p)


def _combine_kernel(idx_ref, x_ref, y_hbm, gate_ref, o_ref, slab_ref, sem):
    tm, D = x_ref.shape
    nchunk = D // LANES
    i = pl.program_id(0)
    n_tok = idx_ref.shape[0] // TOP_K

    def row_dma(step, slot, which):
        for k in range(TOP_K):
            fn = _slab_row_dma(idx_ref, k * n_tok + step * tm, y_hbm, slab_ref.at[slot], k * tm, sem.at[slot],
                               nchunk)[which]
            lax.fori_loop(0, tm, fn, 0, unroll=GATHER_UNROLL)

    @pl.when(i == 0)
    def _():
        row_dma(0, 0, 0)

    gate = gate_ref[...]
    for slot in range(2):
        @pl.when((i + 1 < pl.num_programs(0)) & ((i + 1) % 2 == slot))
        def _():
            row_dma(i + 1, slot, 0)

    for slot in range(2):
        @pl.when(i % 2 == slot)
        def _():
            row_dma(i, slot, 1)
            rows = slab_ref.at[slot]
            for c in range(nchunk):
                cs = slice(c * LANES, (c + 1) * LANES)
                o_ref[:, cs] = x_ref[:, cs] + (_slab_lanes(rows, c, tm, nchunk) * gate[:, 2:3]
                                               + _slab_lanes(rows, c, tm, nchunk, row0=tm) * gate[:, 3:4])


def moe_combine(x, y_slab, pair_idx, route):
    T, D = x.shape
    tm = min(COMBINE_TM, T)
    nchunk = D // LANES
    grid_spec = pltpu.PrefetchScalarGridSpec(
        num_scalar_prefetch=1,
        grid=(T // tm,),
        in_specs=[pl.BlockSpec((tm, D), lambda i, idx: (i, 0)),
                  pl.BlockSpec(memory_space=pl.ANY),
                  pl.BlockSpec((tm, LANES), lambda i, idx: (i, 0))],
        out_specs=pl.BlockSpec((tm, D), lambda i, idx: (i, 0)),
        scratch_shapes=[pltpu.VMEM((2, TOP_K * tm * nchunk, LANES), F32), pltpu.SemaphoreType.DMA((2,))],
    )
    return pl.pallas_call(
        _combine_kernel,
        grid_spec=grid_spec,
        out_shape=jax.ShapeDtypeStruct((T, D), F32),
        compiler_params=_cparams(("arbitrary",), 10 * tm * D * 4),
        name="moe_combine",
    )(pair_idx, x, y_slab, route)


def moe_ffn(x, x_slab, g, w_router_p, wg, wu, wd, expert_base):
    T, D = x.shape
    n_assign = T * TOP_K
    route = moe_router(x, g, w_router_p)
    flat_e = route[:, :TOP_K].astype(jnp.int32).reshape(-1)
    order = jnp.argsort(flat_e)
    se = flat_e[order]
    stok = (order // TOP_K).astype(jnp.int32)
    counts = jnp.bincount(flat_e, length=N_EXPERTS)
    padded = (counts + MOE_TM - 1) // MOE_TM * MOE_TM
    pad_end = jnp.cumsum(padded)
    pad_start = pad_end - padded
    start = jnp.cumsum(counts) - counts
    dest = (pad_start[se] + jnp.arange(n_assign) - start[se]).astype(jnp.int32)
    n_blocks = -(-(n_assign + N_EXPERTS * (MOE_TM - 1)) // MOE_TM)
    n_rows = n_blocks * MOE_TM
    block_e = jnp.minimum(jnp.searchsorted(pad_end, jnp.arange(n_blocks) * MOE_TM, side='right'), N_EXPERTS - 1)
    n_live = (pad_end[-1] // MOE_TM).astype(jnp.int32)
    block_info = jnp.concatenate([n_live[None], block_e.astype(jnp.int32) + expert_base])
    row_e = jnp.repeat(block_e, MOE_TM)
    row_j = jnp.arange(n_rows) - pad_start[row_e]
    row_src = jnp.clip(start[row_e] + row_j, 0, n_assign - 1)
    row_tok = jnp.where(row_j < counts[row_e], stok[row_src], 0).astype(jnp.int32)
    dest_flat = dest[jnp.argsort(order)]
    pair_idx = jnp.concatenate([dest_flat[0::2], dest_flat[1::2]])

    y_slab = swiglu_blocks(x_slab, g, wg, wu, wd, block_info, MOE_TM, row_tok=row_tok)
    return moe_combine(x, y_slab, pair_idx, route)


def _rope_tab(pos, rot_dim):
    half = rot_dim // 2
    inv_freq = jnp.power(jnp.float32(ROPE_THETA), -jnp.arange(0, rot_dim, 2, dtype=F32) / rot_dim)
    ang = pos.astype(F32)[..., None] * inv_freq
    cos, sin = jnp.cos(ang), jnp.sin(ang)
    rest = LANES - rot_dim
    one = jnp.ones(pos.shape + (rest,), F32)
    zr = jnp.zeros(pos.shape + (rest,), F32)
    zh = jnp.zeros_like(sin)
    return jnp.stack([jnp.concatenate([cos, cos, one], -1),
                      jnp.concatenate([zh, sin, zr], -1),
                      jnp.concatenate([-sin, zh, zr], -1)])


def _pad_cols(w, n):
    return jnp.pad(w, ((0, 0), (0, n - w.shape[1])))


def kernel(x, positions, attn_norm, w_in, nsa_q_norm, nsa_k_norm, cmp_k_w1, cmp_k_w2, cmp_k_pos, cmp_v_w1, cmp_v_w2, cmp_v_pos, mla_q_a_norm, mla_w_q_up, mla_kv_a_norm, mla_w_kv_up, mla_q_norm, mla_k_norm, nsa_out_norm, mla_out_norm, w_out, ffn_norm, dense_w_gate, dense_w_up, dense_w_down, moe_router, moe_w_gate, moe_w_up, moe_w_down):
    B, S, D = x.shape
    T = B * S
    depth = w_in.shape[0]
    Dh = NSA_HEAD_DIM
    ncp = S // CMP_STRIDE
    kw = CMP_STRIDE * Dh

    rope_n = _rope_tab(positions, NSA_ROT_DIM).reshape(3, T, LANES)
    rope_m = _rope_tab(positions, MLA_ROPE_DIM).reshape(3, T, LANES)
    cmp_idx = jnp.minimum(jnp.arange(ncp) * CMP_STRIDE + CMP_BLOCK - 1, S - 1)
    rope_c = _rope_tab(positions[:, cmp_idx], NSA_ROT_DIM)

    dense_g, dense_u, dense_d = (w.astype(BF16) for w in (dense_w_gate, dense_w_up, dense_w_down))
    moe_g, moe_u, moe_d = (w.reshape((-1,) + w.shape[2:]).astype(BF16) for w in (moe_w_gate, moe_w_up, moe_w_down))

    g0 = NSA_Q_COLS + NSA_KV_COLS
    m0 = g0 + NSA_GATE_COLS
    xt = x.reshape(T, D)
    for layer in range(depth):
        w = w_in[layer]
        w_in_p = jnp.concatenate([w[:, :g0], w[:, m0:m0 + MLA_Q_RANK], w[:, m0 + MLA_Q_RANK:m0 + MLA_Q_RANK + MLA_KV_RANK],
                                  _pad_cols(w[:, m0 + MLA_Q_RANK + MLA_KV_RANK:], LANES),
                                  _pad_cols(w[:, g0:m0], LANES)], axis=1).astype(BF16)
        proj = rms_matmul(xt, attn_norm[layer][None], w_in_p, min(TM, T), TN_PROJ)

        gk = nsa_k_norm[layer]
        q_n, kc_raw, vc_raw, ks, vs, kwin, vwin = nsa_prep(proj, rope_n, nsa_q_norm[layer][None], gk, B, S)
        kc, vc = nsa_compress(
            kc_raw, vc_raw,
            cmp_k_w1[layer].reshape(2, kw, Dh).astype(BF16), cmp_k_w2[layer].astype(BF16),
            cmp_k_pos[layer].reshape(2, 1, kw),
            cmp_v_w1[layer].reshape(2, kw, Dh).astype(BF16), cmp_v_w2[layer].astype(BF16),
            cmp_v_pos[layer].reshape(2, 1, kw), rope_c, gk)
        o_nsa = nsa_attention(q_n, proj, kc, vc, ks, vs, kwin, vwin, B, S)

        wq = jnp.pad(mla_w_q_up[layer].reshape(MLA_Q_RANK, MLA_HEADS, MLA_QK_DIM),
                     ((0, 0), (0, 0), (0, MLA_QK_PAD - MLA_QK_DIM))).reshape(MLA_Q_RANK, MLA_HEADS * MLA_QK_PAD)
        gq = jnp.pad(mla_q_norm[layer], (0, MLA_QK_PAD - MLA_QK_DIM))[None]
        gkm = jnp.pad(mla_k_norm[layer], (0, MLA_QK_PAD - MLA_QK_DIM))[None]
        q_m = mla_q_prep(proj, mla_q_a_norm[layer][None], wq.astype(BF16), gq, rope_m, B, S)
        k_m, v_m = mla_kv_prep(proj, mla_kv_a_norm[layer][None], mla_w_kv_up[layer].astype(BF16), gkm, rope_m, B, S)
        o_mla = mla_attention(q_m, k_m, v_m)

        is_moe = layer % 2 == 1
        mixed = out_proj(xt, o_nsa, o_mla, nsa_out_norm[layer][None], mla_out_norm[layer][None],
                         w_out[layer].astype(BF16), slab_out=is_moe)

        i = layer // 2
        gf = ffn_norm[layer][None]
        if is_moe:
            xt, xt_slab = mixed
            xt = moe_ffn(xt, xt_slab, gf, _pad_cols(moe_router[i], LANES).astype(BF16), moe_g, moe_u, moe_d,
                         i * N_EXPERTS)
        else:
            xt, = mixed
            tm = min(TM, T)
            info = jnp.concatenate([jnp.full((1,), T // tm, jnp.int32), jnp.full((T // tm,), i, jnp.int32)])
            xt = swiglu_blocks(xt, gf, dense_g, dense_u, dense_d, info, tm)
    return xt.reshape(B, S, D)
```

```python
import functools

import jax
import jax.numpy as jnp
from jax import lax
from jax.experimental import pallas as pl
from jax.experimental.pallas import tpu as pltpu

F32 = jnp.float32
BF16 = jnp.bfloat16

LANES = 128
VMEM_CAP_BYTES = 56 * 1024 * 1024

LOG2E = 1.4426950408889634
NEG = -1e30

NORM_EPS = 1e-6
ROPE_THETA = 500000.0
Q_BLOCK = 128

NSA_HEAD_DIM = 128
NSA_KV_GROUPS = 2
NSA_HEADS_PER_GROUP = 4
NSA_HEADS = NSA_KV_GROUPS * NSA_HEADS_PER_GROUP
NSA_ROT_DIM = NSA_HEAD_DIM // 4
N_NSA_BRANCHES = 3
CMP_BLOCK = 32
CMP_STRIDE = 16
SEL_BLOCK = 64
SEL_SHIFT = SEL_BLOCK.bit_length() - 1
assert 1 << SEL_SHIFT == SEL_BLOCK
N_SEL_BLOCKS = 16
WINDOW = 512
SEL_FORCE = 1.0e4

MLA_V_DIM = 128
MLA_HEADS = 8
MLA_Q_RANK = 512
MLA_KV_RANK = 512
MLA_NOPE_DIM = 128
MLA_ROPE_DIM = 64
MLA_QK_DIM = MLA_NOPE_DIM + MLA_ROPE_DIM
MLA_QK_PAD = 256

N_EXPERTS = 8
TOP_K = 2

NSA_Q_COLS = NSA_HEADS * NSA_HEAD_DIM
NSA_KV_COLS = N_NSA_BRANCHES * 2 * NSA_KV_GROUPS * NSA_HEAD_DIM
NSA_GATE_COLS = NSA_HEADS * N_NSA_BRANCHES
P_Q0 = 0
P_KV0 = P_Q0 + NSA_Q_COLS
P_QA0 = P_KV0 + NSA_KV_COLS
P_KVA0 = P_QA0 + MLA_Q_RANK
P_KR0 = P_KVA0 + MLA_KV_RANK
P_G0 = P_KR0 + LANES
P_COLS = P_G0 + LANES

TM = 512
TN_PROJ = 1280
TN_OUT = 1024
TF = 512
TS_PREP = 1024
TK_ATT = 1024
TQ_MLA = 512
MLA_HEADS_PER_STEP = 4
MOE_TM = 512
COMBINE_TM = 256
GATHER_UNROLL = 8


def _cparams(sem, vmem_bytes):
    return pltpu.CompilerParams(dimension_semantics=sem,
                                vmem_limit_bytes=int(min(VMEM_CAP_BYTES, max(vmem_bytes, 16 * 1024 * 1024))))


def _rms(x, g):
    ms = jnp.mean(x * x, axis=-1, keepdims=True)
    return x * lax.rsqrt(ms + NORM_EPS) * g


def _rope_lanes(x, tab, half):
    return x * tab[0] + pltpu.roll(x, half, 1) * tab[1] + pltpu.roll(x, LANES - half, 1) * tab[2]


def _dot_nt(a, b):
    return lax.dot_general(a, b, (((1,), (1,)), ((), ())), preferred_element_type=F32)


def _rms_matmul_kernel(x_ref, g_ref, w_ref, o_ref, h_ref):
    @pl.when(pl.program_id(1) == 0)
    def _():
        h_ref[...] = _rms(x_ref[...], g_ref[...]).astype(BF16)

    o_ref[...] = jnp.dot(h_ref[...], w_ref[...], preferred_element_type=F32)


def rms_matmul(x, g, w, tm, tn):
    T, K = x.shape
    N = w.shape[1]
    vmem = 2 * tm * K * 4 + tm * K * 2 + 2 * K * tn * 2 + 2 * tm * tn * 4
    return pl.pallas_call(
        _rms_matmul_kernel,
        grid=(T // tm, N // tn),
        in_specs=[pl.BlockSpec((tm, K), lambda i, j: (i, 0)),
                  pl.BlockSpec((1, K), lambda i, j: (0, 0)),
                  pl.BlockSpec((K, tn), lambda i, j: (0, j))],
        out_specs=pl.BlockSpec((tm, tn), lambda i, j: (i, j)),
        out_shape=jax.ShapeDtypeStruct((T, N), F32),
        scratch_shapes=[pltpu.VMEM((tm, K), BF16)],
        compiler_params=_cparams(("parallel", "arbitrary"), 2 * vmem),
        name="rms_matmul",
    )(x, g, w)


def _nsa_prep_kernel(q_in, kc_in, vc_in, ks_in, vs_in, kw_in, vw_in, rope_ref, gq_ref, gk_ref,
                     q_out, kc_out, vc_out, ks_out, vs_out, kw_out, vw_out):
    tab = rope_ref[...]
    half = NSA_ROT_DIM // 2
    Dh = NSA_HEAD_DIM
    fold = Dh ** -0.5 * LOG2E
    for h in range(NSA_HEADS_PER_GROUP):
        qh = _rope_lanes(_rms(q_in[:, h * Dh:(h + 1) * Dh], gq_ref[...]), tab, half)
        q_out[0, 0, h] = (qh * fold).astype(BF16)
    kc_out[0, 0] = kc_in[...]
    vc_out[0, 0] = vc_in[...]
    ks_out[0, 0] = _rope_lanes(_rms(ks_in[...], gk_ref[1:2, :]), tab, half).astype(BF16)
    kw_out[0, 0] = _rope_lanes(_rms(kw_in[...], gk_ref[2:3, :]), tab, half).astype(BF16)
    vs_out[0, 0] = vs_in[...].astype(BF16)
    vw_out[0, 0] = vw_in[...].astype(BF16)


def nsa_prep(proj, rope_n, gq, gk, B, S):
    ts = min(TS_PREP, S)
    ns = S // ts
    G, Hg, Dh = NSA_KV_GROUPS, NSA_HEADS_PER_GROUP, NSA_HEAD_DIM
    kv_blk0 = P_KV0 // Dh

    def in_spec(branch, kv):
        off = kv_blk0 + (branch * 2 + kv) * G
        return pl.BlockSpec((ts, Dh), lambda b, s, g: (b * ns + s, off + g))

    out_spec = pl.BlockSpec((1, 1, ts, Dh), lambda b, s, g: (b, g, s, 0))
    f32_out = jax.ShapeDtypeStruct((B, G, S, Dh), F32)
    bf_out = jax.ShapeDtypeStruct((B, G, S, Dh), BF16)
    return pl.pallas_call(
        _nsa_prep_kernel,
        grid=(B, ns, G),
        in_specs=[pl.BlockSpec((ts, Hg * Dh), lambda b, s, g: (b * ns + s, P_Q0 // (Hg * Dh) + g)),
                  in_spec(0, 0), in_spec(0, 1), in_spec(1, 0), in_spec(1, 1), in_spec(2, 0), in_spec(2, 1),
                  pl.BlockSpec((3, ts, Dh), lambda b, s, g: (0, b * ns + s, 0)),
                  pl.BlockSpec((1, Dh), lambda b, s, g: (0, 0)),
                  pl.BlockSpec((N_NSA_BRANCHES, Dh), lambda b, s, g: (0, 0))],
        out_specs=[pl.BlockSpec((1, 1, Hg, ts, Dh), lambda b, s, g: (b, g, 0, s, 0))] + [out_spec] * 6,
        out_shape=[jax.ShapeDtypeStruct((B, G, Hg, S, Dh), BF16), f32_out, f32_out, bf_out, bf_out, bf_out, bf_out],
        compiler_params=_cparams(("parallel", "parallel", "parallel"), 2 * 2 * 20 * ts * Dh * 4),
        name="nsa_prep",
    )(proj, proj, proj, proj, proj, proj, proj, rope_n, gq, gk)


def _compress_one(t_ref, w1_ref, w2_ref, pos_ref):
    t = t_ref[0, 0]
    ncp = t.shape[0]
    u = jnp.dot((t + pos_ref[0]).astype(BF16), w1_ref[0], preferred_element_type=F32)
    v = jnp.dot((t + pos_ref[1]).astype(BF16), w1_ref[1], preferred_element_type=F32)
    pre = u + pltpu.roll(v, ncp - 1, 0)
    return jnp.dot(jax.nn.gelu(pre).astype(BF16), w2_ref[...], preferred_element_type=F32)


def _compress_kernel(tk_ref, tv_ref, w1k_ref, w2k_ref, posk_ref, w1v_ref, w2v_ref, posv_ref, rope_ref, gk_ref,
                     kc_ref, vc_ref):
    k = _compress_one(tk_ref, w1k_ref, w2k_ref, posk_ref)
    kc_ref[0, 0] = _rope_lanes(_rms(k, gk_ref[0:1, :]), rope_ref[:, 0], NSA_ROT_DIM // 2).astype(BF16)
    vc_ref[0, 0] = _compress_one(tv_ref, w1v_ref, w2v_ref, posv_ref).astype(BF16)


def nsa_compress(kc_raw, vc_raw, w1k, w2k, posk, w1v, w2v, posv, rope_c, gk):
    B, G, S, Dh = kc_raw.shape
    ncp = S // CMP_STRIDE
    kw = CMP_STRIDE * Dh
    tk = kc_raw.reshape(B, G, ncp, kw)
    tv = vc_raw.reshape(B, G, ncp, kw)
    t_spec = pl.BlockSpec((1, 1, ncp, kw), lambda b, g: (b, g, 0, 0))
    w1_spec = pl.BlockSpec((2, kw, Dh), lambda b, g: (0, 0, 0))
    w2_spec = pl.BlockSpec((Dh, Dh), lambda b, g: (0, 0))
    pos_spec = pl.BlockSpec((2, 1, kw), lambda b, g: (0, 0, 0))
    out_spec = pl.BlockSpec((1, 1, ncp, Dh), lambda b, g: (b, g, 0, 0))
    out = jax.ShapeDtypeStruct((B, G, ncp, Dh), BF16)
    return pl.pallas_call(
        _compress_kernel,
        grid=(B, G),
        in_specs=[t_spec, t_spec, w1_spec, w2_spec, pos_spec, w1_spec, w2_spec, pos_spec,
                  pl.BlockSpec((3, 1, ncp, Dh), lambda b, g: (0, b, 0, 0)),
                  pl.BlockSpec((N_NSA_BRANCHES, Dh), lambda b, g: (0, 0))],
        out_specs=[out_spec, out_spec],
        out_shape=[out, out],
        compiler_params=_cparams(("parallel", "parallel"), 2 * (4 * ncp * kw * 4 + 8 * kw * Dh * 2)),
        name="nsa_compress",
    )(tk, tv, w1k, w2k, posk, w1v, w2v, posv, rope_c, gk)


def _online_softmax(s, m, l):
    m_new = jnp.maximum(m, jnp.max(s, axis=-1, keepdims=True))
    alpha = jnp.exp2(m - m_new)
    p = jnp.exp2(s - m_new)
    return p, m_new, alpha * l + jnp.sum(p, axis=-1, keepdims=True), alpha


def _softmax_rows(s):
    p = jnp.exp2(s - jnp.max(s, axis=-1, keepdims=True))
    return p, jnp.sum(p, axis=-1, keepdims=True)


def _nsa_attn_kernel(q_ref, gl_ref, kc_ref, vc_ref, ks_ref, vs_ref, kw_ref, vw_ref, o_ref,
                     *, seq, n_sel):
    G, Hg, Dh, QB = NSA_KV_GROUPS, NSA_HEADS_PER_GROUP, NSA_HEAD_DIM, Q_BLOCK
    R = Hg * QB
    s0 = pl.program_id(1) * QB
    ncp = kc_ref.shape[2]
    nb = max(LANES, seq // SEL_BLOCK)
    tk = TK_ATT
    wk = WINDOW + QB

    t_q = s0 + lax.broadcasted_iota(jnp.int32, (QB, 1), 0)
    t_l = s0 + lax.broadcasted_iota(jnp.int32, (1, QB), 1)

    cmp_end = lax.broadcasted_iota(jnp.int32, (1, ncp), 1) * CMP_STRIDE + (CMP_BLOCK - 1)
    bias_c = jnp.where(cmp_end <= t_q, 0.0, NEG)
    row_ok = jnp.where(t_q >= CMP_BLOCK - 1, 1.0, 0.0)
    w0 = pl.multiple_of(jnp.maximum(s0 - WINDOW, 0), QB)
    diff = t_q - (w0 + lax.broadcasted_iota(jnp.int32, (1, wk), 1))
    bias_w = jnp.where((diff >= 0) & (diff < WINDOW), 0.0, NEG)

    j_i = lax.broadcasted_iota(jnp.int32, (nb, ncp), 0) * SEL_BLOCK
    c_i = lax.broadcasted_iota(jnp.int32, (nb, ncp), 1) * CMP_STRIDE
    overlap_t = jnp.where((c_i < j_i + SEL_BLOCK) & (c_i + (CMP_BLOCK - 1) >= j_i), 1.0, 0.0).astype(BF16)
    jt = lax.broadcasted_iota(jnp.int32, (nb, QB), 0)
    cur = jnp.right_shift(t_l, SEL_SHIFT)
    force_t = jnp.where((jt == 0) | (jt == cur) | (jt == cur - 1), SEL_FORCE, 0.0)
    eligible_t = jt <= cur
    jf = jt.astype(F32)

    qbs, sels, o_cs, o_ws = [], [], [], []
    for g in range(G):
        qb = q_ref[0, g].reshape(R, Dh)
        qbs.append(qb)

        p, l = _softmax_rows(_dot_nt(qb, kc_ref[0, g]).reshape(Hg, QB, ncp) + bias_c[None])
        p = p * (row_ok[None] / l)
        o_cs.append(jnp.dot(p.reshape(R, ncp).astype(BF16), vc_ref[0, g], preferred_element_type=F32))

        imp = _dot_nt(overlap_t, jnp.sum(p, axis=0).astype(BF16))
        imp = jnp.where(eligible_t, imp + force_t, -jnp.inf)
        sel = jnp.zeros((nb, QB), F32)
        for _ in range(n_sel):
            mx = jnp.max(imp, axis=0, keepdims=True)
            first = jnp.min(jnp.where(imp == mx, jf, float(nb)), axis=0, keepdims=True)
            pick = (jf == first) & (mx > -jnp.inf)
            sel = jnp.where(pick, 1.0, sel)
            imp = jnp.where(pick, -jnp.inf, imp)
        sels.append(sel.T.astype(BF16))

        p, l = _softmax_rows(_dot_nt(qb, kw_ref[0, g, pl.ds(w0, wk), :]).reshape(Hg, QB, wk) + bias_w[None])
        o_w = jnp.dot(p.reshape(R, wk).astype(BF16), vw_ref[0, g, pl.ds(w0, wk), :], preferred_element_type=F32)
        o_ws.append(o_w / l.reshape(R, 1))

    jb = lax.broadcasted_iota(jnp.int32, (nb, tk), 0)
    kl = lax.broadcasted_iota(jnp.int32, (nb, tk), 1)
    kl_row = lax.broadcasted_iota(jnp.int32, (1, tk), 1)

    def sel_step(kt, carry):
        k0 = pl.multiple_of(kt * tk, tk)
        expand = jnp.where(jb == jnp.right_shift(k0 + kl, SEL_SHIFT), 1.0, 0.0).astype(BF16)
        causal = k0 + kl_row <= t_q
        out = []
        for g in range(G):
            m, l, acc = carry[g]
            chosen = jnp.dot(sels[g], expand, preferred_element_type=F32)
            bias = jnp.where((chosen > 0.5) & causal, 0.0, NEG)
            s = _dot_nt(qbs[g], ks_ref[0, g, pl.ds(k0, tk), :]).reshape(Hg, QB, tk) + bias[None]
            p, m, l, alpha = _online_softmax(s, m, l)
            pv = jnp.dot(p.reshape(R, tk).astype(BF16), vs_ref[0, g, pl.ds(k0, tk), :], preferred_element_type=F32)
            out.append((m, l, alpha * acc + pv.reshape(Hg, QB, Dh)))
        return tuple(out)

    init = tuple((jnp.full((Hg, QB, 1), NEG, F32), jnp.zeros((Hg, QB, 1), F32), jnp.zeros((Hg, QB, Dh), F32))
                 for _ in range(G))
    final = lax.fori_loop(0, (s0 + QB + tk - 1) // tk, sel_step, init)

    gates = jax.nn.sigmoid(gl_ref[...])
    for g in range(G):
        o_s = (final[g][2] / final[g][1]).reshape(R, Dh)
        for h in range(Hg):
            c0 = (g * Hg + h) * N_NSA_BRANCHES
            rs = slice(h * QB, (h + 1) * QB)
            o_ref[:, (g * Hg + h) * Dh:(g * Hg + h + 1) * Dh] = (
                gates[:, c0:c0 + 1] * o_cs[g][rs] + gates[:, c0 + 1:c0 + 2] * o_s[rs]
                + gates[:, c0 + 2:c0 + 3] * o_ws[g][rs])


def nsa_attention(q, proj, kc, vc, ks, vs, kw, vw, B, S):
    G, Hg, Dh, QB = NSA_KV_GROUPS, NSA_HEADS_PER_GROUP, NSA_HEAD_DIM, Q_BLOCK
    nq = S // QB
    ncp = kc.shape[2]
    n_sel = min(N_SEL_BLOCKS, S // SEL_BLOCK)
    qcols = G * Hg * Dh
    kv_spec = pl.BlockSpec((1, G, S, Dh), lambda b, i: (b, 0, 0, 0), pipeline_mode=pl.Buffered(1))
    c_spec = pl.BlockSpec((1, G, ncp, Dh), lambda b, i: (b, 0, 0, 0), pipeline_mode=pl.Buffered(1))
    sw = max(TK_ATT, ncp, WINDOW + QB)
    vmem = G * (4 * S * Dh * 2 + 2 * ncp * Dh * 2) + 10 * G * Hg * QB * sw * 4
    return pl.pallas_call(
        functools.partial(_nsa_attn_kernel, seq=S, n_sel=n_sel),
        grid=(B, nq),
        in_specs=[pl.BlockSpec((1, G, Hg, QB, Dh), lambda b, i: (b, 0, 0, i, 0)),
                  pl.BlockSpec((QB, LANES), lambda b, i: (b * nq + i, P_G0 // LANES)),
                  c_spec, c_spec, kv_spec, kv_spec, kv_spec, kv_spec],
        out_specs=pl.BlockSpec((QB, qcols), lambda b, i: (b * nq + i, 0)),
        out_shape=jax.ShapeDtypeStruct((B * S, qcols), F32),
        compiler_params=_cparams(("parallel", "arbitrary"), vmem),
        name="nsa_attention",
    )(q, proj, kc, vc, ks, vs, kw, vw)


def _mla_q_kernel(qa_ref, ga_ref, w_ref, gq_ref, rope_ref, o_ref):
    W = MLA_QK_PAD
    c = _rms(qa_ref[...], ga_ref[...]).astype(BF16)
    q = jnp.dot(c, w_ref[...], preferred_element_type=F32)
    tab = rope_ref[...]
    fold = MLA_QK_DIM ** -0.5 * LOG2E
    for h in range(MLA_HEADS):
        qh = q[:, h * W:(h + 1) * W]
        ms = jnp.sum(qh * qh, axis=-1, keepdims=True) * (1.0 / MLA_QK_DIM)
        qn = qh * lax.rsqrt(ms + NORM_EPS) * gq_ref[...]
        o_ref[0, h, :, :LANES] = (qn[:, :LANES] * fold).astype(BF16)
        o_ref[0, h, :, LANES:] = (_rope_lanes(qn[:, LANES:], tab, MLA_ROPE_DIM // 2) * fold).astype(BF16)


def mla_q_prep(proj, ga, wq, gq, rope_m, B, S):
    tm = min(TM, S)
    ns = S // tm
    H, R, W = MLA_HEADS, MLA_Q_RANK, MLA_QK_PAD
    return pl.pallas_call(
        _mla_q_kernel,
        grid=(B * ns,),
        in_specs=[pl.BlockSpec((tm, R), lambda r: (r, P_QA0 // R)),
                  pl.BlockSpec((1, R), lambda r: (0, 0)),
                  pl.BlockSpec((R, H * W), lambda r: (0, 0)),
                  pl.BlockSpec((1, W), lambda r: (0, 0)),
                  pl.BlockSpec((3, tm, LANES), lambda r: (0, r, 0))],
        out_specs=pl.BlockSpec((1, H, tm, W), lambda r: (r // ns, 0, r % ns, 0)),
        out_shape=jax.ShapeDtypeStruct((B, H, S, W), BF16),
        compiler_params=_cparams(("parallel",), 2 * (tm * R * 4 + R * H * W * 2 + tm * H * W * 2) + 3 * tm * H * W * 4),
        name="mla_q_prep",
    )(proj, ga, wq, gq, rope_m)


def _mla_kv_kernel(kva_ref, kr_ref, ga_ref, w_ref, gk_ref, rope_ref, k_ref, v_ref):
    W = MLA_QK_PAD
    c = _rms(kva_ref[...], ga_ref[...]).astype(BF16)
    kv = jnp.dot(c, w_ref[...], preferred_element_type=F32)
    tab = rope_ref[...]
    k_rot = kr_ref[...]
    ss_rot = jnp.sum(k_rot * k_rot, axis=-1, keepdims=True)
    for h in range(MLA_HEADS):
        k_nope = kv[:, h * W:h * W + LANES]
        ms = (jnp.sum(k_nope * k_nope, axis=-1, keepdims=True) + ss_rot) * (1.0 / MLA_QK_DIM)
        r = lax.rsqrt(ms + NORM_EPS)
        k_ref[0, h, :, :LANES] = (k_nope * r * gk_ref[:, :LANES]).astype(BF16)
        k_ref[0, h, :, LANES:] = _rope_lanes(k_rot * r * gk_ref[:, LANES:], tab, MLA_ROPE_DIM // 2).astype(BF16)
        v_ref[0, h] = kv[:, h * W + LANES:(h + 1) * W].astype(BF16)


def mla_kv_prep(proj, ga, wkv, gk, rope_m, B, S):
    tm = min(TM, S)
    ns = S // tm
    H, R, W = MLA_HEADS, MLA_KV_RANK, MLA_QK_PAD
    return pl.pallas_call(
        _mla_kv_kernel,
        grid=(B * ns,),
        in_specs=[pl.BlockSpec((tm, R), lambda r: (r, P_KVA0 // R)),
                  pl.BlockSpec((tm, LANES), lambda r: (r, P_KR0 // LANES)),
                  pl.BlockSpec((1, R), lambda r: (0, 0)),
                  pl.BlockSpec((R, H * W), lambda r: (0, 0)),
                  pl.BlockSpec((1, W), lambda r: (0, 0)),
                  pl.BlockSpec((3, tm, LANES), lambda r: (0, r, 0))],
        out_specs=[pl.BlockSpec((1, H, tm, W), lambda r: (r // ns, 0, r % ns, 0)),
                   pl.BlockSpec((1, H, tm, MLA_V_DIM), lambda r: (r // ns, 0, r % ns, 0))],
        out_shape=[jax.ShapeDtypeStruct((B, H, S, W), BF16), jax.ShapeDtypeStruct((B, H, S, MLA_V_DIM), BF16)],
        compiler_params=_cparams(("parallel",), 2 * (tm * R * 4 + R * H * W * 2 + tm * H * W * 3) + 3 * tm * H * W * 4),
        name="mla_kv_prep",
    )(proj, proj, ga, wkv, gk, rope_m)


def _mla_attn_kernel(q_ref, k_ref, v_ref, o_ref, *, tq, heads):
    i = pl.program_id(2)
    Dv = v_ref.shape[-1]

    def tile(k0, carry, bias, width=tq):
        out = []
        for c in range(heads):
            m, l, acc = carry[c]
            s = _dot_nt(q_ref[0, c], k_ref[0, c, pl.ds(k0, width), :])
            if bias is not None:
                s = s + bias
            p, m, l, alpha = _online_softmax(s, m, l)
            pv = jnp.dot(p.astype(BF16), v_ref[0, c, pl.ds(k0, width), :], preferred_element_type=F32)
            out.append((m, l, alpha * acc + pv))
        return tuple(out)

    init = tuple((jnp.full((tq, 1), NEG, F32), jnp.zeros((tq, 1), F32), jnp.zeros((tq, Dv), F32))
                 for _ in range(heads))
    carry = lax.fori_loop(0, i // 2, lambda kt, c: tile(pl.multiple_of(kt * 2 * tq, 2 * tq), c, None, 2 * tq), init)
    carry = lax.cond(i % 2 == 1, lambda c: tile(pl.multiple_of((i - 1) * tq, tq), c, None), lambda c: c, carry)
    causal = jnp.where(lax.broadcasted_iota(jnp.int32, (tq, tq), 1) <= lax.broadcasted_iota(jnp.int32, (tq, tq), 0),
                       0.0, NEG)
    carry = tile(pl.multiple_of(i * tq, tq), carry, causal)
    for c in range(heads):
        o_ref[:, c * Dv:(c + 1) * Dv] = carry[c][2] / carry[c][1]


def mla_attention(q, k, v):
    B, H, S, W = q.shape
    tq = min(TQ_MLA, S)
    nq = S // tq
    Dv = v.shape[-1]
    hp = MLA_HEADS_PER_STEP
    vmem = hp * (2 * (S * W * 2 + S * Dv * 2) + 4 * tq * W * 2 + 8 * tq * tq * 4)
    return pl.pallas_call(
        functools.partial(_mla_attn_kernel, tq=tq, heads=hp),
        grid=(B, H // hp, nq),
        in_specs=[pl.BlockSpec((1, hp, tq, W), lambda b, h, i: (b, h, i, 0)),
                  pl.BlockSpec((1, hp, S, W), lambda b, h, i: (b, h, 0, 0), pipeline_mode=pl.Buffered(1)),
                  pl.BlockSpec((1, hp, S, Dv), lambda b, h, i: (b, h, 0, 0), pipeline_mode=pl.Buffered(1))],
        out_specs=pl.BlockSpec((tq, hp * Dv), lambda b, h, i: (b * nq + i, h)),
        out_shape=jax.ShapeDtypeStruct((B * S, H * Dv), F32),
        compiler_params=_cparams(("parallel", "parallel", "arbitrary"), vmem),
        name="mla_attention",
    )(q, k, v)


def _slab_lanes(ref, c, rows, nchunk, row0=0):
    return ref[pl.ds(row0 * nchunk + c, rows, stride=nchunk), :]


def _slab_row_dma(idx_ref, base, src_hbm, dst_ref, dst_row0, sem, nchunk):
    def copy(r):
        src = src_hbm.at[pl.ds(pl.multiple_of(idx_ref[base + r] * nchunk, nchunk), nchunk)]
        dst = dst_ref.at[pl.ds(pl.multiple_of((dst_row0 + r) * nchunk, nchunk), nchunk)]
        return pltpu.make_async_copy(src, dst, sem)

    def start(r, c):
        copy(r).start()
        return c

    def wait(r, c):
        copy(r).wait()
        return c

    return start, wait


def _out_proj_kernel(x_ref, a_ref, b_ref, ga_ref, gb_ref, wa_ref, wb_ref, o_ref, *rest, slab_out):
    na_ref, nb_ref = rest[-2:]
    j = pl.program_id(1)

    @pl.when(j == 0)
    def _():
        na_ref[...] = _rms(a_ref[...], ga_ref[...]).astype(BF16)
        nb_ref[...] = _rms(b_ref[...], gb_ref[...]).astype(BF16)

    o = (x_ref[...] + jnp.dot(na_ref[...], wa_ref[...], preferred_element_type=F32)
         + jnp.dot(nb_ref[...], wb_ref[...], preferred_element_type=F32))
    o_ref[...] = o
    if slab_out:
        s_ref = rest[0]
        tm, tn = o.shape
        per = tn // LANES
        nchunk = s_ref.shape[0] // tm
        for jj in range(nchunk // per):
            @pl.when(j == jj)
            def _():
                for c in range(per):
                    s_ref[pl.ds(jj * per + c, tm, stride=nchunk), :] = o[:, c * LANES:(c + 1) * LANES]


def out_proj(x, o_nsa, o_mla, g_nsa, g_mla, w_out, slab_out):
    T, D = x.shape
    Ka, Kb = o_nsa.shape[1], o_mla.shape[1]
    assert Ka == Kb
    tm, tn = min(TM, T), TN_OUT
    nchunk = D // LANES
    vmem = 2 * (tm * tn * 8 + 2 * tm * Ka * 4 + 2 * Ka * tn * 2) + 2 * tm * Ka * 2
    out_specs = [pl.BlockSpec((tm, tn), lambda i, j: (i, j))]
    out_shape = [jax.ShapeDtypeStruct((T, D), F32)]
    if slab_out:
        out_specs.append(pl.BlockSpec((tm * nchunk, LANES), lambda i, j: (i, 0)))
        out_shape.append(jax.ShapeDtypeStruct((T * nchunk, LANES), F32))
        vmem += 2 * tm * D * 4
    return pl.pallas_call(
        functools.partial(_out_proj_kernel, slab_out=slab_out),
        grid=(T // tm, D // tn),
        in_specs=[pl.BlockSpec((tm, tn), lambda i, j: (i, j)),
                  pl.BlockSpec((tm, Ka), lambda i, j: (i, 0)),
                  pl.BlockSpec((tm, Kb), lambda i, j: (i, 0)),
                  pl.BlockSpec((1, Ka), lambda i, j: (0, 0)),
                  pl.BlockSpec((1, Kb), lambda i, j: (0, 0)),
                  pl.BlockSpec((Ka, tn), lambda i, j: (0, j)),
                  pl.BlockSpec((Kb, tn), lambda i, j: (1, j))],
        out_specs=out_specs,
        out_shape=out_shape,
        scratch_shapes=[pltpu.VMEM((tm, Ka), BF16), pltpu.VMEM((tm, Kb), BF16)],
        compiler_params=_cparams(("parallel", "arbitrary"), 2 * vmem),
        name="out_proj",
    )(x, o_nsa, o_mla, g_nsa, g_mla, w_out, w_out)


def _swiglu_kernel(*refs, gathered):
    if gathered:
        be_ref, tok_ref, x_hbm, g_ref, wg_ref, wu_ref, wd_ref, o_ref, h_ref, acc_ref, slab_ref, sem = refs
    else:
        be_ref, x_ref, g_ref, wg_ref, wu_ref, wd_ref, o_ref, h_ref, acc_ref = refs
    tm, D = h_ref.shape
    nchunk = D // LANES
    i = pl.program_id(0)
    f = pl.program_id(1)
    live = i < be_ref[0]

    if gathered:
        def row_dma(block, slot):
            return _slab_row_dma(tok_ref, block * tm, x_hbm, slab_ref.at[slot], 0, sem.at[slot], nchunk)

        @pl.when((i == 0) & (f == 0))
        def _():
            lax.fori_loop(0, tm, row_dma(0, 0)[0], 0, unroll=GATHER_UNROLL)

        for slot in range(2):
            @pl.when(live & (f == 0) & (i % 2 == slot))
            def _():
                lax.fori_loop(0, tm, row_dma(i, slot)[1], 0, unroll=GATHER_UNROLL)
                rows = slab_ref.at[slot]
                ss = jnp.zeros((tm, 1), F32)
                for c in range(nchunk):
                    xc = _slab_lanes(rows, c, tm, nchunk)
                    ss = ss + jnp.sum(xc * xc, axis=-1, keepdims=True)
                r = lax.rsqrt(ss * (1.0 / D) + NORM_EPS)
                for c in range(nchunk):
                    cs = slice(c * LANES, (c + 1) * LANES)
                    h_ref[:, cs] = (_slab_lanes(rows, c, tm, nchunk) * r * g_ref[:, cs]).astype(BF16)
                acc_ref[...] = jnp.zeros_like(acc_ref)

            @pl.when((f == 1) & (i + 1 < be_ref[0]) & ((i + 1) % 2 == slot))
            def _():
                lax.fori_loop(0, tm, row_dma(i + 1, slot)[0], 0, unroll=GATHER_UNROLL)
    else:
        @pl.when(live & (f == 0))
        def _():
            h_ref[...] = _rms(x_ref[...], g_ref[...]).astype(BF16)
            acc_ref[...] = jnp.zeros_like(acc_ref)

    @pl.when(live)
    def _():
        h = h_ref[...]
        a = jax.nn.silu(jnp.dot(h, wg_ref[0], preferred_element_type=F32)) * jnp.dot(
            h, wu_ref[0], preferred_element_type=F32)
        acc_ref[...] += jnp.dot(a.astype(BF16), wd_ref[0], preferred_element_type=F32)

    last = f == pl.num_programs(1) - 1

    @pl.when(live & last)
    def _():
        if gathered:
            for c in range(nchunk):
                o_ref[pl.ds(c, tm, stride=nchunk), :] = acc_ref[:, c * LANES:(c + 1) * LANES]
        else:
            o_ref[...] = x_ref[...] + acc_ref[...]

    if gathered:
        @pl.when(jnp.logical_not(live) & last)
        def _():
            o_ref[...] = jnp.zeros_like(o_ref)


def swiglu_blocks(x, g, wg, wu, wd, block_info, tm, row_tok=None):
    gathered = row_tok is not None
    D = wg.shape[1]
    F = wg.shape[2]
    nchunk = D // LANES
    R = row_tok.shape[0] if gathered else x.shape[0]
    tf = TF
    nf = F // tf
    nsp = 2 if gathered else 1

    def wcol(i, f, be, *_):
        live = i < be[0]
        return (be[1 + i], 0, jnp.where(live, f, nf - 1))

    def wrow(i, f, be, *_):
        live = i < be[0]
        return (be[1 + i], jnp.where(live, f, nf - 1), 0)

    w_specs = [pl.BlockSpec((1, D), lambda i, f, *_: (0, 0)),
               pl.BlockSpec((1, D, tf), wcol),
               pl.BlockSpec((1, D, tf), wcol),
               pl.BlockSpec((1, tf, D), wrow)]
    scratch = [pltpu.VMEM((tm, D), BF16), pltpu.VMEM((tm, D), F32)]
    vmem = 4 * tm * D * 4 + tm * D * 2 + tm * D * 4 + 2 * 3 * D * tf * 2 + 6 * tm * tf * 4
    if gathered:
        in_specs = [pl.BlockSpec(memory_space=pl.ANY)] + w_specs
        out_spec = pl.BlockSpec((tm * nchunk, LANES), lambda i, f, *_: (i, 0))
        out_shape = jax.ShapeDtypeStruct((R * nchunk, LANES), F32)
        scratch += [pltpu.VMEM((2, tm * nchunk, LANES), F32), pltpu.SemaphoreType.DMA((2,))]
        prefetch = (block_info, row_tok)
    else:
        in_specs = [pl.BlockSpec((tm, D), lambda i, f, *_: (i, 0))] + w_specs
        out_spec = pl.BlockSpec((tm, D), lambda i, f, *_: (i, 0))
        out_shape = jax.ShapeDtypeStruct((R, D), F32)
        prefetch = (block_info,)
    grid_spec = pltpu.PrefetchScalarGridSpec(
        num_scalar_prefetch=nsp,
        grid=(R // tm, nf),
        in_specs=in_specs,
        out_specs=out_spec,
        scratch_shapes=scratch,
    )
    return pl.pallas_call(
        functools.partial(_swiglu_kernel, gathered=gathered),
        grid_spec=grid_spec,
        out_shape=out_shape,
        compiler_params=_cparams(("arbitrary" if gathered else "parallel", "arbitrary"), vmem + (4 << 20)),
        name="swiglu",
    )(*prefetch, x, g, wg, wu, wd)


def _router_kernel(x_ref, g_ref, w_ref, o_ref):
    h = _rms(x_ref[...], g_ref[...]).astype(BF16)
    logits = jnp.dot(h, w_ref[...], preferred_element_type=F32)
    lane = lax.broadcasted_iota(jnp.int32, logits.shape, 1)
    lf = lane.astype(F32)
    lg = jnp.where(lane < N_EXPERTS, logits, -jnp.inf)
    m1 = jnp.max(lg, axis=-1, keepdims=True)
    i1 = jnp.min(jnp.where(lg == m1, lf, float(LANES)), axis=-1, keepdims=True)
    lg2 = jnp.where(lf == i1, -jnp.inf, lg)
    m2 = jnp.max(lg2, axis=-1, keepdims=True)
    i2 = jnp.min(jnp.where(lg2 == m2, lf, float(LANES)), axis=-1, keepdims=True)
    e2 = jnp.exp(m2 - m1)
    den = 1.0 + e2
    o_ref[...] = jnp.where(lane == 0, i1, jnp.where(lane == 1, i2, jnp.where(lane == 2, 1.0 / den, e2 / den)))


def moe_router(x, g, w_router_p):
    T, D = x.shape
    tm = min(TM, T)
    return pl.pallas_call(
        _router_kernel,
        grid=(T // tm,),
        in_specs=[pl.BlockSpec((tm, D), lambda i: (i, 0)),
                  pl.BlockSpec((1, D), lambda i: (0, 0)),
                  pl.BlockSpec((D, LANES), lambda i: (0, 0))],
        out_specs=pl.BlockSpec((tm, LANES), lambda i: (i, 0)),
        out_shape=jax.ShapeDtypeStruct((T, LANES), F32),
        compiler_params=_cparams(("parallel",), 4 * tm * D * 4),
        name="moe_router",
    )(x, g, w_router_p)


def _combine_kernel(idx_ref, x_ref, y_hbm, gate_ref, o_ref, slab_ref, sem):
    tm, D = x_ref.shape
    nchunk = D // LANES
    i = pl.program_id(0)
    n_tok = idx_ref.shape[0] // TOP_K

    def row_dma(step, slot, which):
        for k in range(TOP_K):
            fn = _slab_row_dma(idx_ref, k * n_tok + step * tm, y_hbm, slab_ref.at[slot], k * tm, sem.at[slot],
                               nchunk)[which]
            lax.fori_loop(0, tm, fn, 0, unroll=GATHER_UNROLL)

    @pl.when(i == 0)
    def _():
        row_dma(0, 0, 0)

    gate = gate_ref[...]
    for slot in range(2):
        @pl.when((i + 1 < pl.num_programs(0)) & ((i + 1) % 2 == slot))
        def _():
            row_dma(i + 1, slot, 0)

    for slot in range(2):
        @pl.when(i % 2 == slot)
        def _():
            row_dma(i, slot, 1)
            rows = slab_ref.at[slot]
            for c in range(nchunk):
                cs = slice(c * LANES, (c + 1) * LANES)
                o_ref[:, cs] = x_ref[:, cs] + (_slab_lanes(rows, c, tm, nchunk) * gate[:, 2:3]
                                               + _slab_lanes(rows, c, tm, nchunk, row0=tm) * gate[:, 3:4])


def moe_combine(x, y_slab, pair_idx, route):
    T, D = x.shape
    tm = min(COMBINE_TM, T)
    nchunk = D // LANES
    grid_spec = pltpu.PrefetchScalarGridSpec(
        num_scalar_prefetch=1,
        grid=(T // tm,),
        in_specs=[pl.BlockSpec((tm, D), lambda i, idx: (i, 0)),
                  pl.BlockSpec(memory_space=pl.ANY),
                  pl.BlockSpec((tm, LANES), lambda i, idx: (i, 0))],
        out_specs=pl.BlockSpec((tm, D), lambda i, idx: (i, 0)),
        scratch_shapes=[pltpu.VMEM((2, TOP_K * tm * nchunk, LANES), F32), pltpu.SemaphoreType.DMA((2,))],
    )
    return pl.pallas_call(
        _combine_kernel,
        grid_spec=grid_spec,
        out_shape=jax.ShapeDtypeStruct((T, D), F32),
        compiler_params=_cparams(("arbitrary",), 10 * tm * D * 4),
        name="moe_combine",
    )(pair_idx, x, y_slab, route)


def moe_ffn(x, x_slab, g, w_router_p, wg, wu, wd, expert_base):
    T, D = x.shape
    n_assign = T * TOP_K
    route = moe_router(x, g, w_router_p)
    flat_e = route[:, :TOP_K].astype(jnp.int32).reshape(-1)
    order = jnp.argsort(flat_e)
    se = flat_e[order]
    stok = (order // TOP_K).astype(jnp.int32)
    counts = jnp.bincount(flat_e, length=N_EXPERTS)
    padded = (counts + MOE_TM - 1) // MOE_TM * MOE_TM
    pad_end = jnp.cumsum(padded)
    pad_start = pad_end - padded
    start = jnp.cumsum(counts) - counts
    dest = (pad_start[se] + jnp.arange(n_assign) - start[se]).astype(jnp.int32)
    n_blocks = -(-(n_assign + N_EXPERTS * (MOE_TM - 1)) // MOE_TM)
    n_rows = n_blocks * MOE_TM
    block_e = jnp.minimum(jnp.searchsorted(pad_end, jnp.arange(n_blocks) * MOE_TM, side='right'), N_EXPERTS - 1)
    n_live = (pad_end[-1] // MOE_TM).astype(jnp.int32)
    block_info = jnp.concatenate([n_live[None], block_e.astype(jnp.int32) + expert_base])
    row_e = jnp.repeat(block_e, MOE_TM)
    row_j = jnp.arange(n_rows) - pad_start[row_e]
    row_src = jnp.clip(start[row_e] + row_j, 0, n_assign - 1)
    row_tok = jnp.where(row_j < counts[row_e], stok[row_src], 0).astype(jnp.int32)
    dest_flat = dest[jnp.argsort(order)]
    pair_idx = jnp.concatenate([dest_flat[0::2], dest_flat[1::2]])

    y_slab = swiglu_blocks(x_slab, g, wg, wu, wd, block_info, MOE_TM, row_tok=row_tok)
    return moe_combine(x, y_slab, pair_idx, route)


def _rope_tab(pos, rot_dim):
    half = rot_dim // 2
    inv_freq = jnp.power(jnp.float32(ROPE_THETA), -jnp.arange(0, rot_dim, 2, dtype=F32) / rot_dim)
    ang = pos.astype(F32)[..., None] * inv_freq
    cos, sin = jnp.cos(ang), jnp.sin(ang)
    rest = LANES - rot_dim
    one = jnp.ones(pos.shape + (rest,), F32)
    zr = jnp.zeros(pos.shape + (rest,), F32)
    zh = jnp.zeros_like(sin)
    return jnp.stack([jnp.concatenate([cos, cos, one], -1),
                      jnp.concatenate([zh, sin, zr], -1),
                      jnp.concatenate([-sin, zh, zr], -1)])


def _pad_cols(w, n):
    return jnp.pad(w, ((0, 0), (0, n - w.shape[1])))


def kernel(x, positions, attn_norm, w_in, nsa_q_norm, nsa_k_norm, cmp_k_w1, cmp_k_w2, cmp_k_pos, cmp_v_w1, cmp_v_w2, cmp_v_pos, mla_q_a_norm, mla_w_q_up, mla_kv_a_norm, mla_w_kv_up, mla_q_norm, mla_k_norm, nsa_out_norm, mla_out_norm, w_out, ffn_norm, dense_w_gate, dense_w_up, dense_w_down, moe_router, moe_w_gate, moe_w_up, moe_w_down):
    B, S, D = x.shape
    T = B * S
    depth = w_in.shape[0]
    Dh = NSA_HEAD_DIM
    ncp = S // CMP_STRIDE
    kw = CMP_STRIDE * Dh

    rope_n = _rope_tab(positions, NSA_ROT_DIM).reshape(3, T, LANES)
    rope_m = _rope_tab(positions, MLA_ROPE_DIM).reshape(3, T, LANES)
    cmp_idx = jnp.minimum(jnp.arange(ncp) * CMP_STRIDE + CMP_BLOCK - 1, S - 1)
    rope_c = _rope_tab(positions[:, cmp_idx], NSA_ROT_DIM)

    dense_g, dense_u, dense_d = (w.astype(BF16) for w in (dense_w_gate, dense_w_up, dense_w_down))
    moe_g, moe_u, moe_d = (w.reshape((-1,) + w.shape[2:]).astype(BF16) for w in (moe_w_gate, moe_w_up, moe_w_down))

    g0 = NSA_Q_COLS + NSA_KV_COLS
    m0 = g0 + NSA_GATE_COLS
    xt = x.reshape(T, D)
    for layer in range(depth):
        w = w_in[layer]
        w_in_p = jnp.concatenate([w[:, :g0], w[:, m0:m0 + MLA_Q_RANK], w[:, m0 + MLA_Q_RANK:m0 + MLA_Q_RANK + MLA_KV_RANK],
                                  _pad_cols(w[:, m0 + MLA_Q_RANK + MLA_KV_RANK:], LANES),
                                  _pad_cols(w[:, g0:m0], LANES)], axis=1).astype(BF16)
        proj = rms_matmul(xt, attn_norm[layer][None], w_in_p, min(TM, T), TN_PROJ)

        gk = nsa_k_norm[layer]
        q_n, kc_raw, vc_raw, ks, vs, kwin, vwin = nsa_prep(proj, rope_n, nsa_q_norm[layer][None], gk, B, S)
        kc, vc = nsa_compress(
            kc_raw, vc_raw,
            cmp_k_w1[layer].reshape(2, kw, Dh).astype(BF16), cmp_k_w2[layer].astype(BF16),
            cmp_k_pos[layer].reshape(2, 1, kw),
            cmp_v_w1[layer].reshape(2, kw, Dh).astype(BF16), cmp_v_w2[layer].astype(BF16),
            cmp_v_pos[layer].reshape(2, 1, kw), rope_c, gk)
        o_nsa = nsa_attention(q_n, proj, kc, vc, ks, vs, kwin, vwin, B, S)

        wq = jnp.pad(mla_w_q_up[layer].reshape(MLA_Q_RANK, MLA_HEADS, MLA_QK_DIM),
                     ((0, 0), (0, 0), (0, MLA_QK_PAD - MLA_QK_DIM))).reshape(MLA_Q_RANK, MLA_HEADS * MLA_QK_PAD)
        gq = jnp.pad(mla_q_norm[layer], (0, MLA_QK_PAD - MLA_QK_DIM))[None]
        gkm = jnp.pad(mla_k_norm[layer], (0, MLA_QK_PAD - MLA_QK_DIM))[None]
        q_m = mla_q_prep(proj, mla_q_a_norm[layer][None], wq.astype(BF16), gq, rope_m, B, S)
        k_m, v_m = mla_kv_prep(proj, mla_kv_a_norm[layer][None], mla_w_kv_up[layer].astype(BF16), gkm, rope_m, B, S)
        o_mla = mla_attention(q_m, k_m, v_m)

        is_moe = layer % 2 == 1
        mixed = out_proj(xt, o_nsa, o_mla, nsa_out_norm[layer][None], mla_out_norm[layer][None],
                         w_out[layer].astype(BF16), slab_out=is_moe)

        i = layer // 2
        gf = ffn_norm[layer][None]
        if is_moe:
            xt, xt_slab = mixed
            xt = moe_ffn(xt, xt_slab, gf, _pad_cols(moe_router[i], LANES).astype(BF16), moe_g, moe_u, moe_d,
                         i * N_EXPERTS)
        else:
            xt, = mixed
            tm = min(TM, T)
            info = jnp.concatenate([jnp.full((1,), T // tm, jnp.int32), jnp.full((T // tm,), i, jnp.int32)])
            xt = swiglu_blocks(xt, gf, dense_g, dense_u, dense_d, info, tm)
    return xt.reshape(B, S, D)
```
